```python
import math
import jax, jax.numpy as jnp
from jax import lax
import numpy as np

D_MODEL = 2048
BATCH = 2
SEQ = 4096
DEPTH = 4
DEC_BATCH = 32
DEC_SEQ = 1
PAST_LEN = 16384
PAGE_SIZE = 128

HEAD_DIM = 64
A_HEADS = D_MODEL // (4 * HEAD_DIM)
A_KV_HEADS = A_HEADS // 4
A_GROUP = A_HEADS // A_KV_HEADS
A_WINDOW = 128
B_HEADS = D_MODEL // (4 * HEAD_DIM)
B_BRANCHES = ((128, 1), (512, 4), (2048, 16))
B_MAX_WINDOW = 2048
C_HEADS = D_MODEL // (2 * HEAD_DIM)
C_WIDTH = C_HEADS * HEAD_DIM
C_LORA_W = 64
C_LORA_A = 64
C_LORA_G = 128
C_SHIFT_W = 3 * C_WIDTH + C_LORA_W + C_LORA_A + C_LORA_G
A_COLS = (A_HEADS + 2 * A_KV_HEADS) * HEAD_DIM
B_COLS = 3 * B_HEADS * HEAD_DIM
IN_COLS = A_COLS + B_COLS + C_SHIFT_W
MIX_WIDTH = (A_HEADS + B_HEADS + C_HEADS) * HEAD_DIM
MEM_TOKENS = 256
MEM_HEADS = 4
MEM_HEAD_DIM = D_MODEL // 16
MEM_WIDTH = MEM_HEADS * MEM_HEAD_DIM
N_EXPERTS = 32
TOP_K = 4
D_FF = D_MODEL
SWIGLU_ALPHA = 1.702
SWIGLU_LIMIT = 7.0
MOE_BLOCK = 128
MOE_BLOCK_SMALL = 8
BAND_BLOCK = 128
LN_EPS = 1e-5
GN_EPS = 64e-5
DECAY_SCALE = math.exp(-0.5)
DEEPNORM_ALPHA = (2 * DEPTH) ** 0.25
DEEPNORM_BETA = (8 * DEPTH) ** -0.25

kernel_name = 'hybrid_swa_dilated_rwkv7_moe_step'


def layer_norm(x, g, b):
    xf = x.astype(jnp.float32)
    mu = jnp.mean(xf, axis=-1, keepdims=True)
    var = jnp.mean(jnp.square(xf - mu), axis=-1, keepdims=True)
    return ((xf - mu) * lax.rsqrt(var + LN_EPS) * g + b).astype(x.dtype)


def deepnorm_residual(x, h, g, b):
    return layer_norm(DEEPNORM_ALPHA * x + h.astype(x.dtype), g, b)


def _softmax_parts(s, sink):
    m = jnp.max(s, axis=-1)
    if sink is not None:
        sink = sink.astype(jnp.float32)
        m = jnp.maximum(m, sink)
    p = jnp.exp(s - m[..., None])
    denom = jnp.sum(p, axis=-1)
    if sink is not None:
        denom = denom + jnp.exp(sink - m)
    return p, denom, m + jnp.log(denom)


def banded_attention(q, k, v, span, sink=None):
    n, l, hk, g, hd = q.shape
    blk = BAND_BLOCK
    nb = -(-l // blk)
    pad = nb * blk - l
    qb = jnp.pad(q, ((0, 0), (0, pad), (0, 0), (0, 0), (0, 0))).reshape(n, nb, blk, hk, g, hd)
    kp = jnp.pad(k, ((0, 0), (blk, pad), (0, 0), (0, 0))).reshape(n, nb + 1, blk, hk, hd)
    vp = jnp.pad(v, ((0, 0), (blk, pad), (0, 0), (0, 0))).reshape(n, nb + 1, blk, hk, hd)
    kb = jnp.concatenate([kp[:, :-1], kp[:, 1:]], axis=2)
    vb = jnp.concatenate([vp[:, :-1], vp[:, 1:]], axis=2).astype(jnp.float32)
    s = jnp.einsum('nbqhgd,nbkhd->nbhgqk', qb, kb).astype(jnp.float32) * (hd ** -0.5)
    dist = blk + jnp.arange(blk)[:, None] - jnp.arange(2 * blk)[None, :]
    kpos = jnp.arange(nb)[:, None] * blk + jnp.arange(2 * blk)[None, :] - blk
    mask = ((dist >= 0) & (dist <= span))[None] & (kpos >= 0)[:, None, :]
    s = jnp.where(mask[None, :, None, None], s, -jnp.inf)
    p, denom, lse = _softmax_parts(s, sink)
    o = jnp.einsum('nbhgqk,nbkhd->nbqhgd', p, vb) / jnp.moveaxis(denom, -1, 2)[..., None]
    o = o.reshape(n, nb * blk, hk, g, hd)[:, :l]
    lse = jnp.moveaxis(lse, -1, 2).reshape(n, nb * blk, hk, g)[:, :l]
    return o, lse


def gathered_attention(q, k_all, v_all, idx, sink=None):
    hd = q.shape[-1]
    valid = idx >= 0
    safe = jnp.maximum(idx, 0)
    kg = k_all[:, safe]
    vg = v_all[:, safe].astype(jnp.float32)
    s = jnp.einsum('nthgd,ntmhd->nthgm', q, kg).astype(jnp.float32) * (hd ** -0.5)
    s = jnp.where(valid[None, :, None, None, :], s, -jnp.inf)
    p, denom, lse = _softmax_parts(s, sink)
    o = jnp.einsum('nthgm,ntmhd->nthgd', p, vg) / denom[..., None]
    return o, lse


def combine_by_denominator(outs, lses):
    wts = jax.nn.softmax(jnp.stack(lses, axis=0), axis=0)
    return jnp.sum(wts[..., None] * jnp.stack(outs, axis=0), axis=0)


def dilated_branch_prompt(q, k, v, window, dilation):
    n, t, h, hd = q.shape
    ls = t // dilation

    def strided(z):
        return z.reshape(n, ls, dilation, h, hd).transpose(0, 2, 1, 3, 4).reshape(n * dilation, ls, h, hd)

    o, lse = banded_attention(strided(q)[:, :, :, None], strided(k), strided(v), window // dilation)
    o = o[:, :, :, 0].reshape(n, dilation, ls, h, hd).transpose(0, 2, 1, 3, 4).reshape(n, t, h, hd)
    lse = lse[..., 0].reshape(n, dilation, ls, h).transpose(0, 2, 1, 3).reshape(n, t, h)
    return o, lse


def split_projection(x, w_in):
    n, t, _ = x.shape
    sizes = (A_HEADS * HEAD_DIM, A_KV_HEADS * HEAD_DIM, A_KV_HEADS * HEAD_DIM,
             B_HEADS * HEAD_DIM, B_HEADS * HEAD_DIM, B_HEADS * HEAD_DIM, C_SHIFT_W)
    cuts = [int(c) for c in np.cumsum(sizes)[:-1]]
    qa, ka, va, qb, kb, vb, fc = jnp.split(x @ w_in, cuts, axis=-1)
    qa = qa.reshape(n, t, A_KV_HEADS, A_GROUP, HEAD_DIM)
    ka = ka.reshape(n, t, A_KV_HEADS, HEAD_DIM)
    va = va.reshape(n, t, A_KV_HEADS, HEAD_DIM)
    qb = qb.reshape(n, t, B_HEADS, HEAD_DIM)
    kb = kb.reshape(n, t, B_HEADS, HEAD_DIM)
    vb = vb.reshape(n, t, B_HEADS, HEAD_DIM)
    return qa, ka, va, qb, kb, vb, fc


def _heads(z):
    return z.reshape(*z.shape[:-1], C_HEADS, HEAD_DIM).astype(jnp.float32)


def rwkv7_scan(s0, r, decay, k, v, kk, a):
    def step(s, inp):
        r_t, w_t, k_t, v_t, kk_t, a_t = inp
        sa = jnp.einsum('nhij,nhj->nhi', s, kk_t)
        s = (s * w_t[:, :, None, :] - sa[..., None] * (kk_t * a_t)[:, :, None, :]
             + v_t[..., None] * k_t[:, :, None, :])
        return s, jnp.einsum('nhij,nhj->nhi', s, r_t)

    xs = tuple(jnp.moveaxis(z, 1, 0) for z in (r, decay, k, v, kk, a))
    s_final, y = lax.scan(step, s0.astype(jnp.float32), xs)
    return s_final, jnp.moveaxis(y, 0, 1)


def rwkv7_mixer(fc, prev, s0, mu, w0, w_up, a0, a_up, g_up, k_k, k_a, r_k, gn_g, gn_b):
    n, t, _ = fc.shape
    shifted = jnp.concatenate([prev[:, None].astype(fc.dtype), fc[:, :-1]], axis=1)
    f = fc + (shifted - fc) * mu
    cuts = [C_WIDTH, 2 * C_WIDTH, 3 * C_WIDTH, 3 * C_WIDTH + C_LORA_W, 3 * C_WIDTH + C_LORA_W + C_LORA_A]
    r, k, v, wd, ad, gd = jnp.split(f, cuts, axis=-1)
    decay = jnp.exp(-DECAY_SCALE * jax.nn.sigmoid((w0 + jnp.tanh(wd) @ w_up).astype(jnp.float32)))
    a = jax.nn.sigmoid(a0 + ad @ a_up)
    g = (jax.nn.sigmoid(gd) @ g_up).astype(jnp.float32)
    kk = _heads(k * k_k)
    kk = kk / jnp.maximum(jnp.sqrt(jnp.sum(jnp.square(kk), axis=-1, keepdims=True)), 1e-12)
    k = k * (1 + (a - 1) * k_a)
    r_h, k_h, v_h, a_h = _heads(r), _heads(k), _heads(v), _heads(a)
    s_final, y = rwkv7_scan(s0, r_h, _heads(decay), k_h, v_h, kk, a_h)
    y_mu = jnp.mean(y, axis=-1, keepdims=True)
    y_var = jnp.mean(jnp.square(y - y_mu), axis=-1, keepdims=True)
    yn = ((y - y_mu) * lax.rsqrt(y_var + GN_EPS)).reshape(n, t, C_WIDTH) * gn_g + gn_b
    bonus = (jnp.sum(r_h * k_h * r_k, axis=-1, keepdims=True) * v_h).reshape(n, t, C_WIDTH)
    return ((yn + bonus) * g).astype(fc.dtype), s_final.astype(s0.dtype), fc[:, -1]


def merge_heads(oa, ob, oc, w_out):
    n, t = oc.shape[:2]
    h = jnp.concatenate([oa.reshape(n, t, A_HEADS * HEAD_DIM).astype(oc.dtype),
                         ob.reshape(n, t, B_HEADS * HEAD_DIM).astype(oc.dtype), oc], axis=-1)
    return h @ w_out


def memory_attention(x, mem_k, mem_v, w_q, w_o):
    n, t, _ = x.shape
    q = (x @ w_q).reshape(n, t, MEM_HEADS, MEM_HEAD_DIM)
    s = jnp.einsum('nthd,nmhd->nhtm', q, mem_k).astype(jnp.float32) * (MEM_HEAD_DIM ** -0.5)
    p = jax.nn.softmax(s, axis=-1)
    o = jnp.einsum('nhtm,nmhd->nthd', p.astype(mem_v.dtype), mem_v).reshape(n, t, MEM_WIDTH)
    return o @ w_o


def moe_ffn(x, l, router_w, router_b, w_gate_up, b_gate_up, w_down, b_down):
    n, t, d = x.shape
    tok = n * t
    xt = x.reshape(tok, d)
    logits = (xt @ router_w[l] + router_b[l]).astype(jnp.float32)
    top_val, top_idx = lax.top_k(logits, TOP_K)
    gates = jax.nn.softmax(top_val, axis=-1)
    flat_e = top_idx.reshape(-1)
    n_assign = tok * TOP_K
    blk = MOE_BLOCK if n_assign >= MOE_BLOCK * N_EXPERTS else MOE_BLOCK_SMALL
    order = jnp.argsort(flat_e)
    e_sorted = flat_e[order]
    tok_sorted = order // TOP_K
    counts = jnp.zeros((N_EXPERTS,), jnp.int32).at[flat_e].add(1)
    padded = (counts + blk - 1) // blk * blk
    pad_end = jnp.cumsum(padded)
    pad_start = pad_end - padded
    start = jnp.cumsum(counts) - counts
    dest = pad_start[e_sorted] + jnp.arange(n_assign) - start[e_sorted]
    n_blocks = -(-n_assign // blk) + N_EXPERTS
    rows = jnp.zeros((n_blocks * blk, d), x.dtype).at[dest].set(xt[tok_sorted])
    block_e = jnp.minimum(jnp.searchsorted(pad_end, jnp.arange(n_blocks) * blk, side='right'), N_EXPERTS - 1)

    def expert_block(args):
        xb, e = args
        gu = xb @ w_gate_up[l, e] + b_gate_up[l, e]
        gate = jnp.minimum(gu[:, :D_FF], SWIGLU_LIMIT)
        up = jnp.clip(gu[:, D_FF:], -SWIGLU_LIMIT, SWIGLU_LIMIT)
        hid = (up + 1) * gate * jax.nn.sigmoid(SWIGLU_ALPHA * gate)
        return hid @ w_down[l, e] + b_down[l, e]

    y_rows = lax.map(expert_block, (rows.reshape(n_blocks, blk, d), block_e)).reshape(n_blocks * blk, d)
    contrib = y_rows[dest] * gates.reshape(-1)[order][:, None].astype(x.dtype)
    return jax.ops.segment_sum(contrib, tok_sorted, num_segments=tok).reshape(n, t, d)


def _uniform(key, shape, std):
    bound = std * math.sqrt(3.0)
    return jax.random.uniform(key, shape, jnp.float32, -bound, bound)


def setup_inputs(seed: int = 0) -> dict:
    key = jax.random.key(seed)
    ks = iter(jax.random.split(key, 48))
    nk = lambda: next(ks)
    la = min(A_WINDOW, PAST_LEN)
    lb = min(B_MAX_WINDOW, PAST_LEN)
    D = D_MODEL
    return {
        'x_prompt': jax.random.normal(nk(), (BATCH, SEQ, D), jnp.float32),
        'x_sample': jax.random.normal(nk(), (DEC_BATCH, DEC_SEQ, D), jnp.float32),
        'cache_a_k': jax.random.normal(nk(), (DEPTH, DEC_BATCH, la, A_KV_HEADS, HEAD_DIM), jnp.float32),
        'cache_a_v': jax.random.normal(nk(), (DEPTH, DEC_BATCH, la, A_KV_HEADS, HEAD_DIM), jnp.float32),
        'cache_b_k': _uniform(nk(), (DEPTH, DEC_BATCH, lb, B_HEADS, HEAD_DIM), 1.0),
        'cache_b_v': _uniform(nk(), (DEPTH, DEC_BATCH, lb, B_HEADS, HEAD_DIM), 1.0),
        'state_c_wkv': 0.5 * jax.random.normal(nk(), (DEPTH, DEC_BATCH, C_HEADS, HEAD_DIM, HEAD_DIM), jnp.float32),
        'state_c_shift': jax.random.normal(nk(), (DEPTH, DEC_BATCH, C_SHIFT_W), jnp.float32),
        'cache_mem_k': jax.random.normal(nk(), (DEPTH, DEC_BATCH, MEM_TOKENS, MEM_HEADS, MEM_HEAD_DIM), jnp.float32),
        'cache_mem_v': jax.random.normal(nk(), (DEPTH, DEC_BATCH, MEM_TOKENS, MEM_HEADS, MEM_HEAD_DIM), jnp.float32),
        'mem_prompt': jax.random.normal(nk(), (BATCH, MEM_TOKENS, D), jnp.float32),
        'w_in': _uniform(nk(), (DEPTH, D, IN_COLS), D ** -0.5),
        'a_sink': jax.random.normal(nk(), (DEPTH, A_HEADS), jnp.float32),
        'c_mu': jax.random.uniform(nk(), (DEPTH, C_SHIFT_W), jnp.float32, 0.0, 1.0),
        'c_w0': jax.random.normal(nk(), (DEPTH, C_WIDTH), jnp.float32),
        'c_w_up': _uniform(nk(), (DEPTH, C_LORA_W, C_WIDTH), 0.5 * C_LORA_W ** -0.5),
        'c_a0': jax.random.normal(nk(), (DEPTH, C_WIDTH), jnp.float32),
        'c_a_up': _uniform(nk(), (DEPTH, C_LORA_A, C_WIDTH), 0.5 * C_LORA_A ** -0.5),
        'c_g_up': _uniform(nk(), (DEPTH, C_LORA_G, C_WIDTH), C_LORA_G ** -0.5),
        'c_k_k': 0.85 + 0.05 * jax.random.normal(nk(), (DEPTH, C_WIDTH), jnp.float32),
        'c_k_a': 1.0 + 0.05 * jax.random.normal(nk(), (DEPTH, C_WIDTH), jnp.float32),
        'c_r_k': 0.1 * jax.random.normal(nk(), (DEPTH, C_HEADS, HEAD_DIM), jnp.float32),
        'c_gn_g': 1.0 + 0.05 * jax.random.normal(nk(), (DEPTH, C_WIDTH), jnp.float32),
        'c_gn_b': 0.01 * jax.random.normal(nk(), (DEPTH, C_WIDTH), jnp.float32),
        'w_out': _uniform(nk(), (DEPTH, MIX_WIDTH, D), DEEPNORM_BETA * MIX_WIDTH ** -0.5),
        'ln_g': 1.0 + 0.05 * jax.random.normal(nk(), (DEPTH, 3, D), jnp.float32),
        'ln_b': 0.01 * jax.random.normal(nk(), (DEPTH, 3, D), jnp.float32),
        'w_mem_q': _uniform(nk(), (DEPTH, D, MEM_WIDTH), D ** -0.5),
        'w_mem_k': _uniform(nk(), (DEPTH, D, MEM_WIDTH), D ** -0.5),
        'w_mem_v': _uniform(nk(), (DEPTH, D, MEM_WIDTH), D ** -0.5),
        'w_mem_o': _uniform(nk(), (DEPTH, MEM_WIDTH, D), DEEPNORM_BETA * MEM_WIDTH ** -0.5),
        'router_w': _uniform(nk(), (DEPTH, D, N_EXPERTS), D ** -0.5),
        'router_b': 0.01 * jax.random.normal(nk(), (DEPTH, N_EXPERTS), jnp.float32),
        'w_gate_up': _uniform(nk(), (DEPTH, N_EXPERTS, D, 2 * D_FF), D ** -0.5),
        'b_gate_up': 0.01 * jax.random.normal(nk(), (DEPTH, N_EXPERTS, 2 * D_FF), jnp.float32),
        'w_down': _uniform(nk(), (DEPTH, N_EXPERTS, D_FF, D), DEEPNORM_BETA * D_FF ** -0.5),
        'b_down': 0.01 * jax.random.normal(nk(), (DEPTH, N_EXPERTS, D), jnp.float32),
    }


def reference(x_prompt, x_sample, cache_a_k, cache_a_v, cache_b_k, cache_b_v, state_c_wkv, state_c_shift,
              cache_mem_k, cache_mem_v, mem_prompt, w_in, a_sink, c_mu, c_w0, c_w_up, c_a0, c_a_up, c_g_up,
              c_k_k, c_k_a, c_r_k, c_gn_g, c_gn_b, w_out, ln_g, ln_b, w_mem_q, w_mem_k, w_mem_v, w_mem_o,
              router_w, router_b, w_gate_up, b_gate_up, w_down, b_down):
    xp, xs = x_prompt, x_sample
    n_p, t_p, _ = xp.shape
    n_s, t_s, _ = xs.shape
    n_mem = mem_prompt.shape[1]
    la = cache_a_k.shape[2]
    lb = cache_b_k.shape[2]
    keep_a = min(A_WINDOW, t_p)
    keep_b = min(B_MAX_WINDOW, t_p)
    idx_a = la + jnp.arange(t_s)[:, None] - jnp.arange(A_WINDOW + 1)[None, :]
    idx_b = [lb + jnp.arange(t_s)[:, None] - jnp.arange(w // d + 1)[None, :] * d for w, d in B_BRANCHES]
    layer_states = []
    for l in range(DEPTH):
        cpar = (c_mu[l], c_w0[l], c_w_up[l], c_a0[l], c_a_up[l], c_g_up[l], c_k_k[l], c_k_a[l], c_r_k[l],
                c_gn_g[l], c_gn_b[l])
        sink = a_sink[l].reshape(A_KV_HEADS, A_GROUP)

        qa, ka, va, qb, kb, vb, fc = split_projection(xp, w_in[l])
        oa, _ = banded_attention(qa, ka, va, A_WINDOW, sink[None, None, :, :, None])
        br = [dilated_branch_prompt(qb, kb, vb, w, d) for w, d in B_BRANCHES]
        ob = combine_by_denominator([o for o, _ in br], [s for _, s in br])
        oc, p_wkv, p_shift = rwkv7_mixer(fc, jnp.zeros((n_p, C_SHIFT_W), xp.dtype),
                                         jnp.zeros((n_p, C_HEADS, HEAD_DIM, HEAD_DIM), jnp.float32), *cpar)
        xp = deepnorm_residual(xp, merge_heads(oa, ob, oc, w_out[l]), ln_g[l, 0], ln_b[l, 0])
        mk = (mem_prompt @ w_mem_k[l]).reshape(n_p, n_mem, MEM_HEADS, MEM_HEAD_DIM)
        mv = (mem_prompt @ w_mem_v[l]).reshape(n_p, n_mem, MEM_HEADS, MEM_HEAD_DIM)
        xp = deepnorm_residual(xp, memory_attention(xp, mk, mv, w_mem_q[l], w_mem_o[l]), ln_g[l, 1], ln_b[l, 1])
        xp = deepnorm_residual(xp, moe_ffn(xp, l, router_w, router_b, w_gate_up, b_gate_up, w_down, b_down),
                               ln_g[l, 2], ln_b[l, 2])
        p_items = (ka[:, t_p - keep_a:], va[:, t_p - keep_a:], kb[:, t_p - keep_b:], vb[:, t_p - keep_b:],
                   p_wkv, p_shift, mk, mv)

        qa, ka, va, qb, kb, vb, fc = split_projection(xs, w_in[l])
        ka_all = jnp.concatenate([cache_a_k[l].astype(ka.dtype), ka], axis=1)
        va_all = jnp.concatenate([cache_a_v[l].astype(va.dtype), va], axis=1)
        oa, _ = gathered_attention(qa, ka_all, va_all, idx_a, sink)
        kb_all = jnp.concatenate([cache_b_k[l].astype(kb.dtype), kb], axis=1)
        vb_all = jnp.concatenate([cache_b_v[l].astype(vb.dtype), vb], axis=1)
        br = [gathered_attention(qb[:, :, :, None], kb_all, vb_all, idx) for idx in idx_b]
        ob = combine_by_denominator([o[:, :, :, 0] for o, _ in br], [s[..., 0] for _, s in br])
        oc, s_wkv, s_shift = rwkv7_mixer(fc, state_c_shift[l], state_c_wkv[l], *cpar)
        xs = deepnorm_residual(xs, merge_heads(oa, ob, oc, w_out[l]), ln_g[l, 0], ln_b[l, 0])
        xs = deepnorm_residual(xs, memory_attention(xs, cache_mem_k[l].astype(xs.dtype), cache_mem_v[l].astype(xs.dtype),
                                                    w_mem_q[l], w_mem_o[l]), ln_g[l, 1], ln_b[l, 1])
        xs = deepnorm_residual(xs, moe_ffn(xs, l, router_w, router_b, w_gate_up, b_gate_up, w_down, b_down),
                               ln_g[l, 2], ln_b[l, 2])
        layer_states.append(p_items + (ka, va, kb, vb, s_wkv, s_shift))

    stacked = [jnp.stack(z, axis=0) for z in zip(*layer_states)]
    (p_a_k, p_a_v, p_b_k, p_b_v, p_c_wkv, p_c_shift, p_mem_k, p_mem_v,
     s_a_k, s_a_v, s_b_k, s_b_v, s_c_wkv, s_c_shift) = stacked
    y_prompt = xp
    y_sample = xs
    return (y_prompt, y_sample, p_a_k, p_a_v, p_b_k, p_b_v, p_c_wkv, p_c_shift, p_mem_k, p_mem_v,
            s_a_k, s_a_v, s_b_k, s_b_v, s_c_wkv, s_c_shift)
```

```python
import functools
import math

import jax
import jax.numpy as jnp
from jax import lax
from jax.experimental import pallas as pl
from jax.experimental.pallas import tpu as pltpu

F32 = jnp.float32
BF16 = jnp.bfloat16

D_MODEL = 2048
DEPTH = 4
HEAD_DIM = 64
A_HEADS = 8
A_KV_HEADS = 2
A_GROUP = 4
A_WINDOW = 128
B_HEADS = 8
B_BRANCHES = ((128, 1), (512, 4), (2048, 16))
B_MAX_WINDOW = 2048
C_HEADS = 16
C_WIDTH = C_HEADS * HEAD_DIM
C_LORA_W = 64
C_LORA_A = 64
C_LORA_G = 128
C_SHIFT_W = 3 * C_WIDTH + C_LORA_W + C_LORA_A + C_LORA_G
A_COLS = (A_HEADS + 2 * A_KV_HEADS) * HEAD_DIM
B_COLS = 3 * B_HEADS * HEAD_DIM
IN_COLS = A_COLS + B_COLS + C_SHIFT_W
MEM_TOKENS = 256
MEM_HEADS = 4
MEM_HEAD_DIM = 128
MEM_WIDTH = MEM_HEADS * MEM_HEAD_DIM
N_EXPERTS = 32
TOP_K = 4
D_FF = D_MODEL
SWIGLU_ALPHA = 1.702
SWIGLU_LIMIT = 7.0
BAND = 128
LN_EPS = 1e-5
GN_EPS = 64e-5
DECAY_SCALE = math.exp(-0.5)
DEEPNORM_ALPHA = (2 * DEPTH) ** 0.25

LANES = 128
SUBLANES = 8
VMEM_LIMIT = 48 * 1024 * 1024
SCAN_BLOCK = 128
MOE_BLOCK = 256


def _params(*sem):
    return pltpu.CompilerParams(dimension_semantics=sem, vmem_limit_bytes=VMEM_LIMIT)


def _round_up(x, m):
    return (x + m - 1) // m * m


def _sigmoid(x):
    return 1.0 / (1.0 + jnp.exp(-x))


def _layer_norm(z, g, b):
    mu = jnp.mean(z, axis=-1, keepdims=True)
    d = z - mu
    var = jnp.mean(d * d, axis=-1, keepdims=True)
    return d * lax.rsqrt(var + LN_EPS) * g + b


def _split3(x):
    h = x.astype(BF16)
    r = x - h.astype(F32)
    m = r.astype(BF16)
    l = (r - m.astype(F32)).astype(BF16)
    return h, m, l


def _dot_sel(x, sel):
    h, m, l = _split3(x)
    d = lambda a: jnp.dot(a, sel, preferred_element_type=F32)
    return d(h) + d(m) + d(l)


def _block_ones(width, seg=HEAD_DIM):
    i = jnp.arange(width) // seg
    return (i[:, None] == i[None, :]).astype(BF16)


def _mm_body(a_ref, b_ref, o_ref):
    o_ref[...] = jnp.dot(a_ref[...], b_ref[...], preferred_element_type=F32)


def _matmul(a, b, tm, tn):
    m, k = a.shape
    n = b.shape[1]
    mp = _round_up(m, tm)
    if mp != m:
        a = jnp.pad(a, ((0, mp - m), (0, 0)))
    out = pl.pallas_call(
        _mm_body,
        grid=(n // tn, mp // tm),
        in_specs=[pl.BlockSpec((tm, k), lambda j, i: (i, 0)),
                  pl.BlockSpec((k, tn), lambda j, i: (0, j))],
        out_specs=pl.BlockSpec((tm, tn), lambda j, i: (i, j)),
        out_shape=jax.ShapeDtypeStruct((mp, n), F32),
        compiler_params=_params("parallel", "parallel"),
        name="mm",
    )(a, b)
    return out[:m]


def _mm_res_ln_body(a_ref, b_ref, res_ref, g_ref, beta_ref, o_ref):
    h = jnp.dot(a_ref[...], b_ref[...], preferred_element_type=F32)
    o_ref[...] = _layer_norm(DEEPNORM_ALPHA * res_ref[...] + h, g_ref[...], beta_ref[...])


def _matmul_res_ln(a, b, res, g, beta, tm):
    m, k = a.shape
    d = b.shape[1]
    mp = _round_up(m, tm)
    if mp != m:
        a = jnp.pad(a, ((0, mp - m), (0, 0)))
        res = jnp.pad(res, ((0, mp - m), (0, 0)))
    out = pl.pallas_call(
        _mm_res_ln_body,
        grid=(mp // tm,),
        in_specs=[pl.BlockSpec((tm, k), lambda i: (i, 0)),
                  pl.BlockSpec((k, d), lambda i: (0, 0)),
                  pl.BlockSpec((tm, d), lambda i: (i, 0)),
                  pl.BlockSpec((1, d), lambda i: (0, 0)),
                  pl.BlockSpec((1, d), lambda i: (0, 0))],
        out_specs=pl.BlockSpec((tm, d), lambda i: (i, 0)),
        out_shape=jax.ShapeDtypeStruct((mp, d), F32),
        compiler_params=_params("parallel"),
        name="mm_res_ln",
    )(a, b, res, g.reshape(1, d), beta.reshape(1, d))
    return out[:m]


def _band_attn_body(q_ref, kp_ref, kc_ref, vp_ref, vc_ref, sink_ref, o_ref, lse_ref):
    b = pl.program_id(1)
    scale = HEAD_DIM ** -0.5
    q = q_ref[0].astype(BF16)
    dims = (((1,), (1,)), ((), ()))
    s_p = lax.dot_general(q, kp_ref[0].astype(BF16), dims, preferred_element_type=F32) * scale
    s_c = lax.dot_general(q, kc_ref[0].astype(BF16), dims, preferred_element_type=F32) * scale
    row = lax.broadcasted_iota(jnp.int32, (BAND, BAND), 0)
    col = lax.broadcasted_iota(jnp.int32, (BAND, BAND), 1)
    neg = -jnp.inf
    s_p = jnp.where(col >= row, s_p, neg) + jnp.where(b > 0, 0.0, neg)
    s_c = jnp.where(col <= row, s_c, neg)
    sink = sink_ref[0][:, 0:1]
    m = jnp.maximum(jnp.max(s_p, axis=-1, keepdims=True), jnp.max(s_c, axis=-1, keepdims=True))
    m = jnp.maximum(m, sink)
    p_p = jnp.exp(s_p - m)
    p_c = jnp.exp(s_c - m)
    denom = (jnp.sum(p_p, axis=-1, keepdims=True) + jnp.sum(p_c, axis=-1, keepdims=True)
             + jnp.exp(sink - m))
    num = (jnp.dot(p_p.astype(BF16), vp_ref[0].astype(BF16), preferred_element_type=F32)
           + jnp.dot(p_c.astype(BF16), vc_ref[0].astype(BF16), preferred_element_type=F32))
    o_ref[0] = num / denom
    lse_ref[0] = m + jnp.log(denom)


def _band_attention(q, k, v, sink, group):
    g, l, hd = q.shape
    nb = l // BAND
    sink_arr = jnp.broadcast_to(sink.astype(F32)[:, None, None], (g, 1, LANES))
    qspec = pl.BlockSpec((1, BAND, hd), lambda i, b: (i, b, 0))
    prev = pl.BlockSpec((1, BAND, hd), lambda i, b: (i // group, jnp.maximum(b - 1, 0), 0))
    cur = pl.BlockSpec((1, BAND, hd), lambda i, b: (i // group, b, 0))
    return pl.pallas_call(
        _band_attn_body,
        grid=(g, nb),
        in_specs=[qspec, prev, cur, prev, cur, pl.BlockSpec((1, 1, LANES), lambda i, b: (i, 0, 0))],
        out_specs=[qspec, pl.BlockSpec((1, BAND, 1), lambda i, b: (i, b, 0))],
        out_shape=[jax.ShapeDtypeStruct((g, l, hd), F32), jax.ShapeDtypeStruct((g, l, 1), F32)],
        compiler_params=_params("parallel", "parallel"),
        name="band_attn",
    )(q, k, k, v, v, sink_arr)


def _prompt_attention_a(qa, ka, va, sink):
    n, t = qa.shape[:2]
    q = qa.transpose(0, 2, 3, 1, 4).reshape(n * A_HEADS, t, HEAD_DIM)
    k = ka.transpose(0, 2, 1, 3).reshape(n * A_KV_HEADS, t, HEAD_DIM)
    v = va.transpose(0, 2, 1, 3).reshape(n * A_KV_HEADS, t, HEAD_DIM)
    o, _ = _band_attention(q, k, v, jnp.tile(sink, n), A_GROUP)
    return o.reshape(n, A_HEADS, t, HEAD_DIM).transpose(0, 2, 1, 3).reshape(n, t, A_HEADS * HEAD_DIM)


def _prompt_attention_b(qb, kb, vb):
    n, t, h, hd = qb.shape
    outs, lses = [], []
    for window, dil in B_BRANCHES:
        assert window // dil == BAND
        ls = t // dil

        def strided(z):
            return z.reshape(n, ls, dil, h, hd).transpose(0, 2, 3, 1, 4).reshape(n * dil * h, ls, hd)

        o, lse = _band_attention(strided(qb), strided(kb), strided(vb),
                                 jnp.full((n * dil * h,), -jnp.inf, F32), 1)
        outs.append(o.reshape(n, dil, h, ls, hd).transpose(0, 3, 1, 2, 4).reshape(n, t, h, hd))
        lses.append(lse.reshape(n, dil, h, ls).transpose(0, 3, 1, 2).reshape(n, t, h))
    return outs, lses


def _combine_body(o0_ref, o1_ref, o2_ref, l0_ref, l1_ref, l2_ref, o_ref):
    l0, l1, l2 = l0_ref[...], l1_ref[...], l2_ref[...]
    m = jnp.maximum(jnp.maximum(l0, l1), l2)
    e0, e1, e2 = jnp.exp(l0 - m), jnp.exp(l1 - m), jnp.exp(l2 - m)
    tot = e0 + e1 + e2
    o_ref[...] = (e0 / tot) * o0_ref[...] + (e1 / tot) * o1_ref[...] + (e2 / tot) * o2_ref[...]


def _combine_branches(outs, lses):
    n, t, h, hd = outs[0].shape
    rows = n * t
    tm = 512
    o2 = [o.reshape(rows, h * hd) for o in outs]
    l2 = [jnp.repeat(l.reshape(rows, h), hd, axis=1) for l in lses]
    spec = pl.BlockSpec((tm, h * hd), lambda i: (i, 0))
    out = pl.pallas_call(
        _combine_body,
        grid=(rows // tm,),
        in_specs=[spec] * 6,
        out_specs=spec,
        out_shape=jax.ShapeDtypeStruct((rows, h * hd), F32),
        compiler_params=_params("parallel"),
        name="combine_branches",
    )(*o2, *l2)
    return out.reshape(n, t, h * hd)


def _decode_attn_body(q_ref, kc_ref, vc_ref, kn_ref, vn_ref, sink_ref, bd_ref, o_ref, *, branches, group):
    scale = HEAD_DIM ** -0.5
    bd = bd_ref[...]
    k_new = kn_ref[0]
    v_new = vn_ref[0]
    width = k_new.shape[-1]
    for g in range(group):
        q = q_ref[0, g:g + 1, :]
        sink = sink_ref[g:g + 1, :]
        s_n = _dot_sel(jnp.broadcast_to(k_new * q, (SUBLANES, width)), bd)[0:1] * scale
        outs, lses = [], []
        for start, stride in branches:
            if stride == 1:
                k_c = kc_ref[0, pl.ds(start, BAND), :]
                v_c = vc_ref[0, pl.ds(start, BAND), :]
            else:
                k_c = kc_ref[0, pl.ds(start, BAND, stride=stride), :]
                v_c = vc_ref[0, pl.ds(start, BAND, stride=stride), :]
            s_c = _dot_sel(k_c * q, bd) * scale
            m = jnp.maximum(jnp.maximum(jnp.max(s_c, axis=0, keepdims=True), s_n), sink)
            p_c = jnp.exp(s_c - m)
            p_n = jnp.exp(s_n - m)
            denom = jnp.sum(p_c, axis=0, keepdims=True) + p_n + jnp.exp(sink - m)
            num = jnp.sum(p_c * v_c, axis=0, keepdims=True) + p_n * v_new
            outs.append(num / denom)
            lses.append(m + jnp.log(denom))
        if len(branches) == 1:
            o = outs[0]
        else:
            m = functools.reduce(jnp.maximum, lses)
            es = [jnp.exp(l - m) for l in lses]
            tot = functools.reduce(lambda a, b: a + b, es)
            o = functools.reduce(lambda a, b: a + b, [(e / tot) * o_b for e, o_b in zip(es, outs)])
        o_ref[0, g:g + 1, :] = o


def _decode_attention(q, k_cache, v_cache, k_new, v_new, sink, branches):
    n, group, width = q.shape
    lc = k_cache.shape[1]
    body = functools.partial(_decode_attn_body, branches=branches, group=group)
    cache = pl.BlockSpec((1, lc, LANES), lambda i, c: (i, 0, c))
    new = pl.BlockSpec((1, 1, LANES), lambda i, c: (i, 0, c))
    qspec = pl.BlockSpec((1, group, LANES), lambda i, c: (i, 0, c))
    return pl.pallas_call(
        body,
        grid=(n, width // LANES),
        in_specs=[qspec, cache, cache, new, new,
                  pl.BlockSpec((group, LANES), lambda i, c: (0, c)),
                  pl.BlockSpec((LANES, LANES), lambda i, c: (0, 0))],
        out_specs=qspec,
        out_shape=jax.ShapeDtypeStruct((n, group, width), F32),
        compiler_params=_params("parallel", "parallel"),
        name="decode_attn",
    )(q, k_cache, v_cache, k_new, v_new, sink, _block_ones(LANES))


def _rwkv_pre_body(fc_ref, sh_ref, mu_ref, w0_ref, wup_ref, a0_ref, aup_ref, gup_ref, kk_w_ref, ka_w_ref,
                   bd_ref, r_ref, w_ref, k_ref, v_ref, kk_ref, kka_ref, g_ref):
    fc = fc_ref[...]
    f = fc + (sh_ref[...] - fc) * mu_ref[...]
    c = C_WIDTH
    r = f[:, 0:c]
    k = f[:, c:2 * c]
    v = f[:, 2 * c:3 * c]
    wd = f[:, 3 * c:3 * c + C_LORA_W]
    ad = f[:, 3 * c + C_LORA_W:3 * c + C_LORA_W + C_LORA_A]
    gd = f[:, 3 * c + C_LORA_W + C_LORA_A:]
    dot = lambda x, w_ref_: jnp.dot(x.astype(BF16), w_ref_[...], preferred_element_type=F32)
    decay = jnp.exp(-DECAY_SCALE * _sigmoid(w0_ref[...] + dot(jnp.tanh(wd), wup_ref)))
    a = _sigmoid(a0_ref[...] + dot(ad, aup_ref))
    g = dot(_sigmoid(gd), gup_ref)
    kk = k * kk_w_ref[...]
    norm = jnp.sqrt(_dot_sel(kk * kk, bd_ref[...]))
    kk = kk / jnp.maximum(norm, 1e-12)
    r_ref[...] = r
    w_ref[...] = decay
    k_ref[...] = k * (1.0 + (a - 1.0) * ka_w_ref[...])
    v_ref[...] = v
    kk_ref[...] = kk
    kka_ref[...] = kk * a
    g_ref[...] = g


def _rwkv_pre(fc, shifted, mu, w0, w_up, a0, a_up, g_up, k_k, k_a, tm):
    rows = fc.shape[0]
    c = C_WIDTH
    row = lambda w: pl.BlockSpec((tm, w), lambda i: (i, 0))
    full = lambda a, b: pl.BlockSpec((a, b), lambda i: (0, 0))
    return pl.pallas_call(
        _rwkv_pre_body,
        grid=(rows // tm,),
        in_specs=[row(C_SHIFT_W), row(C_SHIFT_W), full(1, C_SHIFT_W), full(1, c), full(C_LORA_W, c),
                  full(1, c), full(C_LORA_A, c), full(C_LORA_G, c), full(1, c), full(1, c), full(c, c)],
        out_specs=[row(c)] * 7,
        out_shape=[jax.ShapeDtypeStruct((rows, c), F32)] * 7,
        compiler_params=_params("parallel"),
        name="rwkv_pre",
    )(fc, shifted, mu.reshape(1, -1), w0.reshape(1, c), w_up.astype(BF16), a0.reshape(1, c),
      a_up.astype(BF16), g_up.astype(BF16), k_k.reshape(1, c), k_a.reshape(1, c), _block_ones(c))


def _rwkv_scan_body(r_ref, w_ref, k_ref, kk_ref, kka_ref, vt_ref, s0_ref, yt_ref, sout_ref, s_scr, y_scr,
                    *, n_tokens):
    tb = pl.program_id(1)

    @pl.when(tb == 0)
    def _():
        s_scr[...] = s0_ref[0]

    y_scr[...] = jnp.zeros_like(y_scr)
    lane = lax.broadcasted_iota(jnp.int32, (1, 1, SCAN_BLOCK), 2)

    def step(u, carry):
        onehot = lane == u
        row = lambda ref: ref[0, :, pl.ds(u, 1), :]
        s = s_scr[...]
        v_col = jnp.sum(jnp.where(onehot, vt_ref[0], 0.0), axis=-1, keepdims=True)
        sa = jnp.sum(s * row(kk_ref), axis=-1, keepdims=True)
        s = s * row(w_ref) - sa * row(kka_ref) + v_col * row(k_ref)
        y = jnp.sum(s * row(r_ref), axis=-1, keepdims=True)
        s_scr[...] = s
        y_scr[...] = jnp.where(onehot, y, y_scr[...])
        return carry

    lax.fori_loop(0, jnp.minimum(SCAN_BLOCK, n_tokens - tb * SCAN_BLOCK), step, 0)
    yt_ref[0] = y_scr[...]

    @pl.when(tb == pl.num_programs(1) - 1)
    def _():
        sout_ref[0] = s_scr[...]


def _rwkv_scan(r, w, k, kk, kka, v, s0, n_tokens):
    n, h, tp, hd = r.shape
    vt = jnp.swapaxes(v, -1, -2)
    rows = pl.BlockSpec((1, h, SCAN_BLOCK, hd), lambda i, t: (i, 0, t, 0))
    cols = pl.BlockSpec((1, h, hd, SCAN_BLOCK), lambda i, t: (i, 0, 0, t))
    state = pl.BlockSpec((1, h, hd, hd), lambda i, t: (i, 0, 0, 0))
    yt, s_out = pl.pallas_call(
        functools.partial(_rwkv_scan_body, n_tokens=n_tokens),
        grid=(n, tp // SCAN_BLOCK),
        in_specs=[rows] * 5 + [cols, state],
        out_specs=[cols, state],
        out_shape=[jax.ShapeDtypeStruct((n, h, hd, tp), F32), jax.ShapeDtypeStruct((n, h, hd, hd), F32)],
        scratch_shapes=[pltpu.VMEM((h, hd, hd), F32), pltpu.VMEM((h, hd, SCAN_BLOCK), F32)],
        compiler_params=_params("parallel", "arbitrary"),
        name="rwkv_scan",
    )(r, w, k, kk, kka, vt, s0)
    return jnp.swapaxes(yt, -1, -2), s_out


def _rwkv_post_body(y_ref, r_ref, k_ref, v_ref, g_ref, rk_ref, gng_ref, gnb_ref, bd_ref, o_ref):
    bd = bd_ref[...]
    y = y_ref[...]
    mu = _dot_sel(y, bd) * (1.0 / HEAD_DIM)
    d = y - mu
    var = _dot_sel(d * d, bd) * (1.0 / HEAD_DIM)
    yn = d * lax.rsqrt(var + GN_EPS) * gng_ref[...] + gnb_ref[...]
    bonus = _dot_sel(r_ref[...] * k_ref[...] * rk_ref[...], bd) * v_ref[...]
    o_ref[...] = (yn + bonus) * g_ref[...]


def _rwkv_post(y, r, k, v, g, r_k, gn_g, gn_b, tm):
    rows, c = y.shape
    row = pl.BlockSpec((tm, c), lambda i: (i, 0))
    par = pl.BlockSpec((1, c), lambda i: (0, 0))
    return pl.pallas_call(
        _rwkv_post_body,
        grid=(rows // tm,),
        in_specs=[row] * 5 + [par] * 3 + [pl.BlockSpec((c, c), lambda i: (0, 0))],
        out_specs=row,
        out_shape=jax.ShapeDtypeStruct((rows, c), F32),
        compiler_params=_params("parallel"),
        name="rwkv_post",
    )(y, r, k, v, g, r_k.reshape(1, c), gn_g.reshape(1, c), gn_b.reshape(1, c), _block_ones(c))


def _rwkv_mixer(fc, shifted, s0, cpar, tm):
    mu, w0, w_up, a0, a_up, g_up, k_k, k_a, r_k, gn_g, gn_b = cpar
    n, t, _ = fc.shape
    rows = n * t
    r, w, k, v, kk, kka, g = _rwkv_pre(fc.reshape(rows, -1), shifted.reshape(rows, -1),
                                       mu, w0, w_up, a0, a_up, g_up, k_k, k_a, tm)
    tp = _round_up(t, SCAN_BLOCK)

    def heads(z):
        z = z.reshape(n, t, C_HEADS, HEAD_DIM).transpose(0, 2, 1, 3)
        return jnp.pad(z, ((0, 0), (0, 0), (0, tp - t), (0, 0))) if tp != t else z

    y, s_out = _rwkv_scan(heads(r), heads(w), heads(k), heads(kk), heads(kka), heads(v), s0, t)
    y = y[:, :, :t].transpose(0, 2, 1, 3).reshape(rows, C_WIDTH)
    out = _rwkv_post(y, r, k, v, g, r_k.reshape(-1), gn_g, gn_b, tm)
    return out.reshape(n, t, C_WIDTH), s_out


def _mem_attn_body(x_ref, wq_ref, mk_ref, mv_ref, wo_ref, g_ref, beta_ref, o_ref):
    x = x_ref[0]
    q = jnp.dot(x.astype(BF16), wq_ref[...], preferred_element_type=F32)
    scale = MEM_HEAD_DIM ** -0.5
    dims = (((1,), (1,)), ((), ()))
    heads = []
    for h in range(MEM_HEADS):
        sl = slice(h * MEM_HEAD_DIM, (h + 1) * MEM_HEAD_DIM)
        s = lax.dot_general(q[:, sl].astype(BF16), mk_ref[0][:, sl].astype(BF16), dims,
                            preferred_element_type=F32) * scale
        m = jnp.max(s, axis=-1, keepdims=True)
        p = jnp.exp(s - m)
        p = p / jnp.sum(p, axis=-1, keepdims=True)
        heads.append(jnp.dot(p.astype(BF16), mv_ref[0][:, sl].astype(BF16), preferred_element_type=F32))
    o = jnp.concatenate(heads, axis=-1)
    h_out = jnp.dot(o.astype(BF16), wo_ref[...], preferred_element_type=F32)
    o_ref[0] = _layer_norm(DEEPNORM_ALPHA * x + h_out, g_ref[...], beta_ref[...])


def _memory_attention(x, mem_k, mem_v, w_q, w_o, g, beta, tm):
    n, t, d = x.shape
    mt = mem_k.shape[1]
    full = lambda a, b: pl.BlockSpec((a, b), lambda i, j: (0, 0))
    xspec = pl.BlockSpec((1, tm, d), lambda i, j: (i, j, 0))
    mem = pl.BlockSpec((1, mt, MEM_WIDTH), lambda i, j: (i, 0, 0))
    return pl.pallas_call(
        _mem_attn_body,
        grid=(n, t // tm),
        in_specs=[xspec, full(d, MEM_WIDTH), mem, mem, full(MEM_WIDTH, d), full(1, d), full(1, d)],
        out_specs=xspec,
        out_shape=jax.ShapeDtypeStruct((n, t, d), F32),
        compiler_params=_params("parallel", "parallel"),
        name="mem_attn",
    )(x, w_q, mem_k, mem_v, w_o, g.reshape(1, d), beta.reshape(1, d))


def _router_body(x_ref, w_ref, b_ref, idx_ref, gate_ref):
    x = x_ref[...]
    w = w_ref[...]
    xh = x.astype(BF16)
    xl = (x - xh.astype(F32)).astype(BF16)
    wh = w.astype(BF16)
    wl = (w - wh.astype(F32)).astype(BF16)
    d = lambda a, b: jnp.dot(a, b, preferred_element_type=F32)
    logits = d(xh, wh) + d(xh, wl) + d(xl, wh) + b_ref[...]
    lane = lax.broadcasted_iota(jnp.int32, logits.shape, 1).astype(F32)
    neg = -jnp.inf
    work = jnp.where(lane < N_EXPERTS, logits, neg)
    idx_out = jnp.zeros(logits.shape, F32)
    val_out = jnp.full(logits.shape, neg, F32)
    for k in range(TOP_K):
        m = jnp.max(work, axis=-1, keepdims=True)
        first = jnp.min(jnp.where(work == m, lane, float(LANES)), axis=-1, keepdims=True)
        idx_out = jnp.where(lane == k, first, idx_out)
        val_out = jnp.where(lane == k, m, val_out)
        work = jnp.where(lane == first, neg, work)
    e = jnp.exp(val_out - jnp.max(val_out, axis=-1, keepdims=True))
    idx_ref[...] = idx_out.astype(jnp.int32)
    gate_ref[...] = e / jnp.sum(e, axis=-1, keepdims=True)


def _router(x, w, b, tm):
    rows, d = x.shape
    wp = jnp.pad(w, ((0, 0), (0, LANES - N_EXPERTS)))
    bp = jnp.pad(b, (0, LANES - N_EXPERTS)).reshape(1, LANES)
    out = pl.BlockSpec((tm, LANES), lambda i: (i, 0))
    idx, gates = pl.pallas_call(
        _router_body,
        grid=(rows // tm,),
        in_specs=[pl.BlockSpec((tm, d), lambda i: (i, 0)),
                  pl.BlockSpec((d, LANES), lambda i: (0, 0)),
                  pl.BlockSpec((1, LANES), lambda i: (0, 0))],
        out_specs=[out, out],
        out_shape=[jax.ShapeDtypeStruct((rows, LANES), jnp.int32), jax.ShapeDtypeStruct((rows, LANES), F32)],
        compiler_params=_params("parallel"),
        name="router",
    )(x, wp, bp)
    return idx[:, :TOP_K], gates[:, :TOP_K]


def _expert_up_body(be_ref, x_ref, wg_ref, wu_ref, bg_ref, bu_ref, h_ref):
    x = x_ref[...]
    gate = jnp.dot(x, wg_ref[0, 0].astype(BF16), preferred_element_type=F32) + bg_ref[0, 0]
    up = jnp.dot(x, wu_ref[0, 0].astype(BF16), preferred_element_type=F32) + bu_ref[0, 0]
    gate = jnp.minimum(gate, SWIGLU_LIMIT)
    up = jnp.clip(up, -SWIGLU_LIMIT, SWIGLU_LIMIT)
    h_ref[...] = ((up + 1.0) * gate * _sigmoid(SWIGLU_ALPHA * gate)).astype(h_ref.dtype)


def _expert_down_body(be_ref, h_ref, w_ref, b_ref, y_ref):
    y_ref[...] = jnp.dot(h_ref[...], w_ref[0, 0].astype(BF16), preferred_element_type=F32) + b_ref[0, 0]


def _expert_ffn(rows, block_e, layer, w_gate_up, b_gate_up, w_down, b_down, tn):
    nr, d = rows.shape
    nb = nr // MOE_BLOCK
    nj = D_FF // tn
    b_gu = b_gate_up.reshape(DEPTH, N_EXPERTS, 1, 2 * D_FF)
    b_dn = b_down.reshape(DEPTH, N_EXPERTS, 1, d)
    hid = pl.pallas_call(
        _expert_up_body,
        grid_spec=pltpu.PrefetchScalarGridSpec(
            num_scalar_prefetch=1,
            grid=(nj, nb),
            in_specs=[pl.BlockSpec((MOE_BLOCK, d), lambda j, i, be: (i, 0)),
                      pl.BlockSpec((1, 1, d, tn), lambda j, i, be: (layer, be[i], 0, j)),
                      pl.BlockSpec((1, 1, d, tn), lambda j, i, be: (layer, be[i], 0, j + nj)),
                      pl.BlockSpec((1, 1, 1, tn), lambda j, i, be: (layer, be[i], 0, j)),
                      pl.BlockSpec((1, 1, 1, tn), lambda j, i, be: (layer, be[i], 0, j + nj))],
            out_specs=pl.BlockSpec((MOE_BLOCK, tn), lambda j, i, be: (i, j))),
        out_shape=jax.ShapeDtypeStruct((nr, D_FF), BF16),
        compiler_params=_params("parallel", "arbitrary"),
        name="expert_up",
    )(block_e, rows, w_gate_up, w_gate_up, b_gu, b_gu)
    nk = d // tn
    return pl.pallas_call(
        _expert_down_body,
        grid_spec=pltpu.PrefetchScalarGridSpec(
            num_scalar_prefetch=1,
            grid=(nk, nb),
            in_specs=[pl.BlockSpec((MOE_BLOCK, D_FF), lambda j, i, be: (i, 0)),
                      pl.BlockSpec((1, 1, D_FF, tn), lambda j, i, be: (layer, be[i], 0, j)),
                      pl.BlockSpec((1, 1, 1, tn), lambda j, i, be: (layer, be[i], 0, j))],
            out_specs=pl.BlockSpec((MOE_BLOCK, tn), lambda j, i, be: (i, j))),
        out_shape=jax.ShapeDtypeStruct((nr, d), F32),
        compiler_params=_params("parallel", "arbitrary"),
        name="expert_down",
    )(block_e, hid, w_down, b_dn)


def _moe_combine_body(x_ref, y_ref, gate_ref, g_ref, beta_ref, o_ref):
    gates = gate_ref[...]
    acc = gates[:, 0:1] * y_ref[0]
    for k in range(1, TOP_K):
        acc = acc + gates[:, k:k + 1] * y_ref[k]
    o_ref[...] = _layer_norm(DEEPNORM_ALPHA * x_ref[...] + acc, g_ref[...], beta_ref[...])


def _moe_combine(x, y_k, gates, g, beta, tm):
    rows, d = x.shape
    gp = jnp.pad(gates, ((0, 0), (0, LANES - TOP_K)))
    xs = pl.BlockSpec((tm, d), lambda i: (i, 0))
    par = pl.BlockSpec((1, d), lambda i: (0, 0))
    return pl.pallas_call(
        _moe_combine_body,
        grid=(rows // tm,),
        in_specs=[xs, pl.BlockSpec((TOP_K, tm, d), lambda i: (0, i, 0)),
                  pl.BlockSpec((tm, LANES), lambda i: (i, 0)), par, par],
        out_specs=xs,
        out_shape=jax.ShapeDtypeStruct((rows, d), F32),
        compiler_params=_params("parallel"),
        name="moe_combine",
    )(x, y_k, gp, g.reshape(1, d), beta.reshape(1, d))


def _moe_layer(x, layer, router_w, router_b, w_gate_up, b_gate_up, w_down, b_down, g, beta):
    rows, d = x.shape
    rp = _round_up(rows, 256)
    xp = jnp.pad(x, ((0, rp - rows), (0, 0))) if rp != rows else x
    top_idx, gates = _router(xp, router_w[layer], router_b[layer], 256)
    top_idx, gates = top_idx[:rows], gates[:rows]
    n_assign = rows * TOP_K
    flat_e = top_idx.reshape(-1)
    order = jnp.argsort(flat_e)
    e_sorted = flat_e[order]
    counts = jnp.zeros((N_EXPERTS,), jnp.int32).at[flat_e].add(1)
    padded = (counts + MOE_BLOCK - 1) // MOE_BLOCK * MOE_BLOCK
    pad_end = jnp.cumsum(padded)
    pad_start = pad_end - padded
    start = jnp.cumsum(counts) - counts
    dest = pad_start[e_sorted] + jnp.arange(n_assign, dtype=jnp.int32) - start[e_sorted]
    n_blocks = -(-n_assign // MOE_BLOCK) + N_EXPERTS
    row_tok = jnp.zeros((n_blocks * MOE_BLOCK,), jnp.int32).at[dest].set((order // TOP_K).astype(jnp.int32))
    block_e = jnp.minimum(jnp.searchsorted(pad_end, jnp.arange(n_blocks, dtype=jnp.int32) * MOE_BLOCK,
                                           side='right'), N_EXPERTS - 1).astype(jnp.int32)
    pos = jnp.zeros((n_assign,), jnp.int32).at[order].set(dest.astype(jnp.int32)).reshape(rows, TOP_K)
    x_rows = x.astype(BF16)[row_tok]
    y_rows = _expert_ffn(x_rows, block_e, layer, w_gate_up, b_gate_up, w_down, b_down, 512)
    y_k = y_rows[pos.T]
    if rp != rows:
        y_k = jnp.pad(y_k, ((0, 0), (0, rp - rows), (0, 0)))
        gates = jnp.pad(gates, ((0, rp - rows), (0, 0)))
    return _moe_combine(xp, y_k, gates, g, beta, 256)[:rows]


def _split_cols(proj):
    sizes = (A_HEADS * HEAD_DIM, A_KV_HEADS * HEAD_DIM, A_KV_HEADS * HEAD_DIM,
             B_HEADS * HEAD_DIM, B_HEADS * HEAD_DIM, B_HEADS * HEAD_DIM, C_SHIFT_W)
    out, c = [], 0
    for s in sizes:
        out.append(proj[..., c:c + s])
        c += s
    return out


def kernel(x_prompt, x_sample, cache_a_k, cache_a_v, cache_b_k, cache_b_v, state_c_wkv, state_c_shift,
           cache_mem_k, cache_mem_v, mem_prompt, w_in, a_sink, c_mu, c_w0, c_w_up, c_a0, c_a_up, c_g_up,
           c_k_k, c_k_a, c_r_k, c_gn_g, c_gn_b, w_out, ln_g, ln_b, w_mem_q, w_mem_k, w_mem_v, w_mem_o,
           router_w, router_b, w_gate_up, b_gate_up, w_down, b_down):
    n_p, t_p, d = x_prompt.shape
    n_s, t_s, _ = x_sample.shape
    assert t_s == 1 and t_p % BAND == 0 and t_p >= B_MAX_WINDOW
    la = cache_a_k.shape[2]
    lb = cache_b_k.shape[2]
    assert la == A_WINDOW and lb == B_MAX_WINDOW
    rows_p = n_p * t_p
    xp = x_prompt.reshape(rows_p, d)
    xs = x_sample.reshape(n_s, d)
    mem_bf = mem_prompt.reshape(n_p * MEM_TOKENS, d).astype(BF16)
    branches_b = tuple((lb - w, dil) for w, dil in B_BRANCHES)
    sink_a_lanes = None
    states = []
    for l in range(DEPTH):
        cpar = (c_mu[l], c_w0[l], c_w_up[l], c_a0[l], c_a_up[l], c_g_up[l], c_k_k[l], c_k_a[l], c_r_k[l],
                c_gn_g[l], c_gn_b[l])
        w_in_bf = w_in[l].astype(BF16)
        w_out_bf = w_out[l].astype(BF16)
        w_q_bf = w_mem_q[l].astype(BF16)
        w_o_bf = w_mem_o[l].astype(BF16)

        proj = _matmul(xp.astype(BF16), w_in_bf, 512, 512).reshape(n_p, t_p, IN_COLS)
        qa, ka, va, qb, kb, vb, fc = _split_cols(proj)
        ka = ka.reshape(n_p, t_p, A_KV_HEADS, HEAD_DIM)
        va = va.reshape(n_p, t_p, A_KV_HEADS, HEAD_DIM)
        kb = kb.reshape(n_p, t_p, B_HEADS, HEAD_DIM)
        vb = vb.reshape(n_p, t_p, B_HEADS, HEAD_DIM)
        oa = _prompt_attention_a(qa.reshape(n_p, t_p, A_KV_HEADS, A_GROUP, HEAD_DIM), ka, va, a_sink[l])
        ob = _combine_branches(*_prompt_attention_b(qb.reshape(n_p, t_p, B_HEADS, HEAD_DIM), kb, vb))
        shifted = jnp.concatenate([jnp.zeros((n_p, 1, C_SHIFT_W), F32), fc[:, :-1]], axis=1)
        oc, p_wkv = _rwkv_mixer(fc, shifted, jnp.zeros((n_p, C_HEADS, HEAD_DIM, HEAD_DIM), F32), cpar, 128)
        h = jnp.concatenate([oa, ob, oc], axis=-1).reshape(rows_p, d).astype(BF16)
        xp = _matmul_res_ln(h, w_out_bf, xp, ln_g[l, 0], ln_b[l, 0], 256)
        mk = _matmul(mem_bf, w_mem_k[l].astype(BF16), 512, 512).reshape(n_p, MEM_TOKENS, MEM_WIDTH)
        mv = _matmul(mem_bf, w_mem_v[l].astype(BF16), 512, 512).reshape(n_p, MEM_TOKENS, MEM_WIDTH)
        xp = _memory_attention(xp.reshape(n_p, t_p, d), mk, mv, w_q_bf, w_o_bf,
                               ln_g[l, 1], ln_b[l, 1], 256).reshape(rows_p, d)
        p_items = (ka[:, t_p - A_WINDOW:], va[:, t_p - A_WINDOW:], kb[:, t_p - B_MAX_WINDOW:],
                   vb[:, t_p - B_MAX_WINDOW:], p_wkv, fc[:, -1],
                   mk.reshape(n_p, MEM_TOKENS, MEM_HEADS, MEM_HEAD_DIM),
                   mv.reshape(n_p, MEM_TOKENS, MEM_HEADS, MEM_HEAD_DIM))

        proj = _matmul(xs.astype(BF16), w_in_bf, 32, 512)
        qa, ka, va, qb, kb, vb, fc = _split_cols(proj)
        q_g = qa.reshape(n_s, A_KV_HEADS, A_GROUP, HEAD_DIM).transpose(0, 2, 1, 3).reshape(
            n_s, A_GROUP, A_KV_HEADS * HEAD_DIM)
        sink_g = jnp.repeat(a_sink[l].reshape(A_KV_HEADS, A_GROUP).T, HEAD_DIM, axis=1)
        oa = _decode_attention(q_g, cache_a_k[l].reshape(n_s, la, -1), cache_a_v[l].reshape(n_s, la, -1),
                               ka[:, None], va[:, None], sink_g, ((0, 1),))
        oa = oa.reshape(n_s, A_GROUP, A_KV_HEADS, HEAD_DIM).transpose(0, 2, 1, 3).reshape(n_s, -1)
        ob = _decode_attention(qb[:, None], cache_b_k[l].reshape(n_s, lb, -1), cache_b_v[l].reshape(n_s, lb, -1),
                               kb[:, None], vb[:, None], jnp.full((1, B_HEADS * HEAD_DIM), -jnp.inf, F32),
                               branches_b).reshape(n_s, -1)
        oc, s_wkv = _rwkv_mixer(fc[:, None], state_c_shift[l][:, None], state_c_wkv[l], cpar, 32)
        h = jnp.concatenate([oa, ob, oc.reshape(n_s, -1)], axis=-1).astype(BF16)
        xs = _matmul_res_ln(h, w_out_bf, xs, ln_g[l, 0], ln_b[l, 0], 32)
        xs_pad = jnp.pad(xs[:, None], ((0, 0), (0, SUBLANES - 1), (0, 0)))
        xs = _memory_attention(xs_pad, cache_mem_k[l].reshape(n_s, MEM_TOKENS, MEM_WIDTH),
                               cache_mem_v[l].reshape(n_s, MEM_TOKENS, MEM_WIDTH), w_q_bf, w_o_bf,
                               ln_g[l, 1], ln_b[l, 1], SUBLANES)[:, 0]
        s_items = (ka.reshape(n_s, 1, A_KV_HEADS, HEAD_DIM), va.reshape(n_s, 1, A_KV_HEADS, HEAD_DIM),
                   kb.reshape(n_s, 1, B_HEADS, HEAD_DIM), vb.reshape(n_s, 1, B_HEADS, HEAD_DIM), s_wkv, fc)

        x_all = _moe_layer(jnp.concatenate([xp, xs], axis=0), l, router_w, router_b, w_gate_up, b_gate_up,
                           w_down, b_down, ln_g[l, 2], ln_b[l, 2])
        xp, xs = x_all[:rows_p], x_all[rows_p:]
        states.append(p_items + s_items)

    stacked = [jnp.stack(z, axis=0) for z in zip(*states)]
    return (xp.reshape(n_p, t_p, d), xs.reshape(n_s, t_s, d), *stacked)
```

```python
import functools
import math

import jax
import jax.numpy as jnp
from jax import lax
from jax.experimental import pallas as pl
from jax.experimental.pallas import tpu as pltpu

F32 = jnp.float32
BF16 = jnp.bfloat16

D_MODEL = 2048
DEPTH = 4
HEAD_DIM = 64
A_HEADS = 8
A_KV_HEADS = 2
A_GROUP = 4
A_WINDOW = 128
B_HEADS = 8
B_BRANCHES = ((128, 1), (512, 4), (2048, 16))
B_MAX_WINDOW = 2048
C_HEADS = 16
C_WIDTH = C_HEADS * HEAD_DIM
C_LORA_W = 64
C_LORA_A = 64
C_LORA_G = 128
C_SHIFT_W = 3 * C_WIDTH + C_LORA_W + C_LORA_A + C_LORA_G
A_COLS = (A_HEADS + 2 * A_KV_HEADS) * HEAD_DIM
B_COLS = 3 * B_HEADS * HEAD_DIM
AB_COLS = A_COLS + B_COLS
MEM_TOKENS = 256
MEM_HEADS = 4
MEM_HEAD_DIM = 128
MEM_WIDTH = MEM_HEADS * MEM_HEAD_DIM
N_EXPERTS = 32
TOP_K = 4
D_FF = D_MODEL
SWIGLU_ALPHA = 1.702
SWIGLU_LIMIT = 7.0
BAND = 128
LN_EPS = 1e-5
GN_EPS = 64e-5
DECAY_SCALE = math.exp(-0.5)
DEEPNORM_ALPHA = (2 * DEPTH) ** 0.25

LANES = 128
SUBLANES = 8
VMEM_LIMIT = 48 * 1024 * 1024
SCAN_BLOCK = 128
RWKV_CHUNK = 64
MOE_BLOCK = 256
PAIR = LANES // HEAD_DIM


def _params(*sem):
    return pltpu.CompilerParams(dimension_semantics=sem, vmem_limit_bytes=VMEM_LIMIT)


def _round_up(x, m):
    return (x + m - 1) // m * m


def _sigmoid(x):
    return 1.0 / (1.0 + jnp.exp(-x))


def _layer_norm(z, g, b):
    mu = jnp.mean(z, axis=-1, keepdims=True)
    d = z - mu
    var = jnp.mean(d * d, axis=-1, keepdims=True)
    return d * lax.rsqrt(var + LN_EPS) * g + b


def _split3(x):
    h = x.astype(BF16)
    r = x - h.astype(F32)
    m = r.astype(BF16)
    l = (r - m.astype(F32)).astype(BF16)
    return h, m, l


def _bdot(a, b):
    return jnp.dot(a.astype(BF16), b.astype(BF16), preferred_element_type=F32)


def _bdot_nt(a, b):
    return lax.dot_general(a.astype(BF16), b.astype(BF16), (((1,), (1,)), ((), ())), preferred_element_type=F32)


def _bmm(a, b):
    return lax.dot_general(a.astype(BF16), b.astype(BF16), (((2,), (1,)), ((0,), (0,))), preferred_element_type=F32)


def _bmm_nt(a, b):
    return lax.dot_general(a.astype(BF16), b.astype(BF16), (((2,), (2,)), ((0,), (0,))), preferred_element_type=F32)


def _bmm_tn(a, b):
    return lax.dot_general(a.astype(BF16), b.astype(BF16), (((1,), (1,)), ((0,), (0,))), preferred_element_type=F32)


def _mm_body(a_ref, b_ref, o_ref):
    o_ref[...] = jnp.dot(a_ref[...], b_ref[...], preferred_element_type=F32)


def _matmul(a, b, tm, tn):
    m, k = a.shape
    n = b.shape[1]
    mp = _round_up(m, tm)
    if mp != m:
        a = jnp.pad(a, ((0, mp - m), (0, 0)))
    out = pl.pallas_call(
        _mm_body,
        grid=(n // tn, mp // tm),
        in_specs=[pl.BlockSpec((tm, k), lambda j, i: (i, 0)),
                  pl.BlockSpec((k, tn), lambda j, i: (0, j))],
        out_specs=pl.BlockSpec((tm, tn), lambda j, i: (i, j)),
        out_shape=jax.ShapeDtypeStruct((mp, n), F32),
        compiler_params=_params("parallel", "parallel"),
        name="mm",
    )(a, b)
    return out[:m]


def _out_proj_body(oa_ref, ob_ref, oc_ref, wa_ref, wb_ref, wc_ref, res_ref, g_ref, beta_ref, o_ref):
    dot = lambda a, w: jnp.dot(a[...], w[...], preferred_element_type=F32)
    h = dot(oa_ref, wa_ref) + dot(ob_ref, wb_ref) + dot(oc_ref, wc_ref)
    o_ref[...] = _layer_norm(DEEPNORM_ALPHA * res_ref[...] + h, g_ref[...], beta_ref[...])


def _out_proj_ln(oa, ob, oc, w, res, g, beta, tm):
    m = oa.shape[0]
    d = w.shape[1]
    ka, kc = oa.shape[1], oc.shape[1]
    row = lambda width: pl.BlockSpec((tm, width), lambda i: (i, 0))
    par = pl.BlockSpec((1, d), lambda i: (0, 0))
    return pl.pallas_call(
        _out_proj_body,
        grid=(m // tm,),
        in_specs=[row(ka), row(ka), row(kc),
                  pl.BlockSpec((ka, d), lambda i: (0, 0)), pl.BlockSpec((ka, d), lambda i: (1, 0)),
                  pl.BlockSpec((kc, d), lambda i: (1, 0)), row(d), par, par],
        out_specs=row(d),
        out_shape=jax.ShapeDtypeStruct((m, d), F32),
        compiler_params=_params("parallel"),
        name="out_proj_ln",
    )(oa, ob, oc, w, w, w, res, g.reshape(1, d), beta.reshape(1, d))


def _band_bias():
    row = lax.broadcasted_iota(jnp.int32, (BAND, 2 * BAND), 0)
    col = lax.broadcasted_iota(jnp.int32, (BAND, 2 * BAND), 1)
    dist = jnp.where(col < BAND, BAND + row - col, row - (col - BAND))
    return jnp.where((dist >= 0) & (dist <= BAND), 0.0, -jnp.inf), col < BAND


def _softmax_pv(s, v_bf, sink=None):
    m = jnp.max(s, axis=-1, keepdims=True)
    if sink is not None:
        m = jnp.maximum(m, sink)
    p = jnp.exp(s - m)
    denom = jnp.sum(p, axis=-1, keepdims=True)
    if sink is not None:
        denom = denom + jnp.exp(sink - m)
    num = jnp.dot(p.astype(BF16), v_bf, preferred_element_type=F32)
    return num / denom, m + jnp.log(denom)


def _swa_body(sink_ref, q_ref, kp_ref, kc_ref, vp_ref, vc_ref, o_ref):
    b = pl.program_id(1)
    scale = HEAD_DIM ** -0.5
    bias, is_prev = _band_bias()
    bias = bias + jnp.where(is_prev, jnp.where(b > 0, 0.0, -jnp.inf), 0.0)
    q = q_ref[0]
    outs = []
    for kv in range(A_KV_HEADS):
        sl = slice(kv * HEAD_DIM, (kv + 1) * HEAD_DIM)
        k_cat = jnp.concatenate([kp_ref[0][:, sl], kc_ref[0][:, sl]], axis=0).astype(BF16)
        v_cat = jnp.concatenate([vp_ref[0][:, sl], vc_ref[0][:, sl]], axis=0).astype(BF16)
        for g in range(A_GROUP):
            h = kv * A_GROUP + g
            s = _bdot_nt(q[:, h * HEAD_DIM:(h + 1) * HEAD_DIM], k_cat) * scale + bias
            o, _ = _softmax_pv(s, v_cat, sink_ref[h])
            outs.append(o)
    o_ref[0] = jnp.concatenate(outs, axis=-1).astype(o_ref.dtype)


def _swa_attention(qkv, sink):
    n, t, _ = qkv.shape
    qw = A_HEADS * HEAD_DIM
    kblk = qw // LANES
    prev = lambda c: pl.BlockSpec((1, BAND, LANES), lambda i, b: (i, jnp.maximum(b - 1, 0), c))
    cur = lambda c: pl.BlockSpec((1, BAND, LANES), lambda i, b: (i, b, c))
    qspec = pl.BlockSpec((1, BAND, qw), lambda i, b: (i, b, 0))
    return pl.pallas_call(
        _swa_body,
        grid=(n, t // BAND),
        in_specs=[pl.BlockSpec(memory_space=pltpu.SMEM), qspec, prev(kblk), cur(kblk), prev(kblk + 1), cur(kblk + 1)],
        out_specs=qspec,
        out_shape=jax.ShapeDtypeStruct((n, t, qw), BF16),
        compiler_params=_params("parallel", "parallel"),
        name="swa_attn",
    )(sink.astype(F32), qkv, qkv, qkv, qkv, qkv)


def _dilated_body(q_ref, kp_ref, kc_ref, vp_ref, vc_ref, o_ref, kcat, vcat, ob_scr, lse_scr):
    span = B_MAX_WINDOW
    sb = pl.program_id(2)
    scale = HEAD_DIM ** -0.5
    kcat[0:span] = kp_ref[0]
    kcat[span:2 * span] = kc_ref[0]
    vcat[0:span] = vp_ref[0]
    vcat[span:2 * span] = vc_ref[0]
    bias0, is_prev = _band_bias()
    no_prev_span = jnp.where(sb > 0, 0.0, -jnp.inf)
    for bi, (window, dil) in enumerate(B_BRANCHES):
        nsub = span // dil // BAND

        def sub(i, carry, bi=bi, dil=dil, nsub=nsub):
            r = i // nsub
            j = i % nsub
            q_start = r + dil * BAND * j
            k_start = span + q_start - dil * BAND
            if dil == 1:
                q_rows, k_rows = pl.ds(q_start, BAND), pl.ds(k_start, 2 * BAND)
            else:
                q_rows = pl.ds(q_start, BAND, stride=dil)
                k_rows = pl.ds(k_start, 2 * BAND, stride=dil)
            q2 = q_ref.at[0][q_rows, :]
            k2 = kcat[k_rows, :]
            v2 = vcat[k_rows, :]
            bias = bias0 + jnp.where(is_prev, jnp.where(j == 0, no_prev_span, 0.0), 0.0)
            outs, lses = [], []
            for hh in range(PAIR):
                sl = slice(hh * HEAD_DIM, (hh + 1) * HEAD_DIM)
                s = _bdot_nt(q2[:, sl], k2[:, sl]) * scale + bias
                o, lse = _softmax_pv(s, v2[:, sl].astype(BF16))
                outs.append(o)
                lses.append(jnp.broadcast_to(lse, (BAND, HEAD_DIM)))
            ob_scr.at[bi][q_rows, :] = jnp.concatenate(outs, axis=-1)
            lse_scr.at[bi][q_rows, :] = jnp.concatenate(lses, axis=-1)
            return carry

        lax.fori_loop(0, span // BAND, sub, 0, unroll=4)
    l0, l1, l2 = lse_scr[0], lse_scr[1], lse_scr[2]
    m = jnp.maximum(jnp.maximum(l0, l1), l2)
    e0, e1, e2 = jnp.exp(l0 - m), jnp.exp(l1 - m), jnp.exp(l2 - m)
    tot = e0 + e1 + e2
    o_ref[0] = ((e0 / tot) * ob_scr[0] + (e1 / tot) * ob_scr[1] + (e2 / tot) * ob_scr[2]).astype(o_ref.dtype)


def _dilated_attention(qkv):
    n, t, _ = qkv.shape
    span = B_MAX_WINDOW
    width = B_HEADS * HEAD_DIM
    q0 = A_COLS // LANES
    nhp = width // LANES
    blk = lambda c0, prev: pl.BlockSpec(
        (1, span, LANES),
        (lambda i, p, s: (i, jnp.maximum(s - 1, 0), c0 + p)) if prev else (lambda i, p, s: (i, s, c0 + p)))
    return pl.pallas_call(
        _dilated_body,
        grid=(n, nhp, t // span),
        in_specs=[blk(q0, False), blk(q0 + nhp, True), blk(q0 + nhp, False),
                  blk(q0 + 2 * nhp, True), blk(q0 + 2 * nhp, False)],
        out_specs=pl.BlockSpec((1, span, LANES), lambda i, p, s: (i, s, p)),
        out_shape=jax.ShapeDtypeStruct((n, t, width), BF16),
        scratch_shapes=[pltpu.VMEM((2 * span, LANES), F32), pltpu.VMEM((2 * span, LANES), F32),
                        pltpu.VMEM((len(B_BRANCHES), span, LANES), F32),
                        pltpu.VMEM((len(B_BRANCHES), span, LANES), F32)],
        compiler_params=_params("parallel", "parallel", "parallel"),
        name="dilated_attn",
    )(qkv, qkv, qkv, qkv, qkv)


def _cache_attn_body(q_ref, kc_ref, vc_ref, kn_ref, vn_ref, sink_ref, o_ref, *, branches, group):
    scale = HEAD_DIM ** -0.5
    k_new = kn_ref[0]
    v_new = vn_ref[0]
    for g in range(group):
        q = q_ref[0, g]
        sink = sink_ref[g]
        s_n = jnp.sum(k_new * q, axis=-1, keepdims=True) * scale
        outs, lses = [], []
        for start, stride in branches:
            rows = pl.ds(start, BAND) if stride == 1 else pl.ds(start, BAND, stride=stride)
            k_c = kc_ref.at[0, 0][rows]
            v_c = vc_ref.at[0, 0][rows]
            s_c = jnp.sum(k_c * q[None], axis=-1, keepdims=True) * scale
            m = jnp.maximum(jnp.maximum(jnp.max(s_c, axis=0), s_n), sink)
            p_c = jnp.exp(s_c - m[None])
            p_n = jnp.exp(s_n - m)
            denom = jnp.sum(p_c, axis=0) + p_n + jnp.exp(sink - m)
            num = jnp.sum(p_c * v_c, axis=0) + p_n * v_new
            outs.append(num / denom)
            lses.append(m + jnp.log(denom))
        if len(branches) == 1:
            o = outs[0]
        else:
            m = functools.reduce(jnp.maximum, lses)
            es = [jnp.exp(l - m) for l in lses]
            tot = functools.reduce(lambda a, b: a + b, es)
            o = functools.reduce(lambda a, b: a + b, [(e / tot) * o_b for e, o_b in zip(es, outs)])
        o_ref[0, g] = o


def _cache_attention(q, k_cache, v_cache, layer, k_new, v_new, sink, branches):
    n, group, hk, hd = q.shape
    lc = k_cache.shape[2]
    cache = pl.BlockSpec((1, 1, lc, hk, hd), lambda i: (layer, i, 0, 0, 0), pipeline_mode=pl.Buffered(1))
    new = pl.BlockSpec((1, hk, hd), lambda i: (i, 0, 0))
    qspec = pl.BlockSpec((1, group, hk, hd), lambda i: (i, 0, 0, 0))
    return pl.pallas_call(
        functools.partial(_cache_attn_body, branches=branches, group=group),
        grid=(n,),
        in_specs=[qspec, cache, cache, new, new, pl.BlockSpec((group, hk, hd), lambda i: (0, 0, 0))],
        out_specs=qspec,
        out_shape=jax.ShapeDtypeStruct((n, group, hk, hd), F32),
        compiler_params=_params("parallel"),
        name="cache_attn",
    )(q, k_cache, v_cache, k_new, v_new, sink)


def _rwkv_pre_body(fc_ref, prev_ref, mu_ref, w0_ref, wup_ref, a0_ref, aup_ref, gup_ref, kk_w_ref, ka_w_ref,
                   r_ref, lw_ref, k_ref, v_ref, kk_ref, kka_ref, g_ref, *, shift_in_kernel):
    fc = fc_ref[0]
    if shift_in_kernel:
        last = jnp.where(pl.program_id(1) > 0, prev_ref[0][SUBLANES - 1:SUBLANES, :], 0.0)
        rowid = lax.broadcasted_iota(jnp.int32, fc.shape, 0)
        shifted = jnp.where(rowid == 0, last, pltpu.roll(fc, 1, 0))
    else:
        shifted = prev_ref[0]
    f = fc + (shifted - fc) * mu_ref[...]
    c = C_WIDTH
    r = f[:, 0:c]
    k = f[:, c:2 * c]
    v = f[:, 2 * c:3 * c]
    wd = f[:, 3 * c:3 * c + C_LORA_W]
    ad = f[:, 3 * c + C_LORA_W:3 * c + C_LORA_W + C_LORA_A]
    gd = f[:, 3 * c + C_LORA_W + C_LORA_A:]
    dot = lambda x, w_ref_: jnp.dot(x.astype(BF16), w_ref_[...], preferred_element_type=F32)
    log_decay = -DECAY_SCALE * _sigmoid(w0_ref[...] + dot(jnp.tanh(wd), wup_ref))
    a = _sigmoid(a0_ref[...] + dot(ad, aup_ref))
    g_ref[0] = dot(_sigmoid(gd), gup_ref)
    kk = k * kk_w_ref[...]
    k2 = k * (1.0 + (a - 1.0) * ka_w_ref[...])
    for h in range(C_HEADS):
        sl = slice(h * HEAD_DIM, (h + 1) * HEAD_DIM)
        kk_h = kk[:, sl]
        kk_h = kk_h / jnp.maximum(jnp.sqrt(jnp.sum(kk_h * kk_h, axis=-1, keepdims=True)), 1e-12)
        r_ref[0, h] = r[:, sl]
        lw_ref[0, h] = log_decay[:, sl]
        k_ref[0, h] = k2[:, sl]
        v_ref[0, h] = v[:, sl]
        kk_ref[0, h] = kk_h
        kka_ref[0, h] = kk_h * a[:, sl]


def _rwkv_pre(fc, shifted, mu, w0, w_up, a0, a_up, g_up, k_k, k_a, tm):
    n, t, _ = fc.shape
    c = C_WIDTH
    in_kernel = shifted is None
    fspec = pl.BlockSpec((1, tm, C_SHIFT_W), lambda i, j: (i, j, 0))
    if in_kernel:
        per = tm // SUBLANES
        pspec = pl.BlockSpec((1, SUBLANES, C_SHIFT_W), lambda i, j: (i, jnp.maximum(j * per - 1, 0), 0))
        shifted = fc
    else:
        pspec = fspec
    full = lambda a, b: pl.BlockSpec((a, b), lambda i, j: (0, 0))
    head = pl.BlockSpec((1, C_HEADS, tm, HEAD_DIM), lambda i, j: (i, 0, j, 0))
    hshape = jax.ShapeDtypeStruct((n, C_HEADS, t, HEAD_DIM), F32)
    return pl.pallas_call(
        functools.partial(_rwkv_pre_body, shift_in_kernel=in_kernel),
        grid=(n, t // tm),
        in_specs=[fspec, pspec, full(1, C_SHIFT_W), full(1, c), full(C_LORA_W, c),
                  full(1, c), full(C_LORA_A, c), full(C_LORA_G, c), full(1, c), full(1, c)],
        out_specs=[head] * 6 + [pl.BlockSpec((1, tm, c), lambda i, j: (i, j, 0))],
        out_shape=[hshape] * 6 + [jax.ShapeDtypeStruct((n, t, c), F32)],
        compiler_params=_params("parallel", "parallel"),
        name="rwkv_pre",
    )(fc, shifted, mu.reshape(1, -1), w0.reshape(1, c), w_up.astype(BF16), a0.reshape(1, c),
      a_up.astype(BF16), g_up.astype(BF16), k_k.reshape(1, c), k_a.reshape(1, c))


def _rwkv_intra_body(r_ref, lw_ref, k_ref, v_ref, kk_ref, kka_ref, q_ref, y1_ref, m_ref, n_ref):
    c = RWKV_CHUNK
    _, hb, tb, hd = r_ref.shape
    nb = hb * (tb // c)
    load = lambda ref: ref[0].reshape(nb, c, hd)
    row = lax.broadcasted_iota(jnp.int32, (nb, c, c), 1)
    col = lax.broadcasted_iota(jnp.int32, (nb, c, c), 2)
    incl = col <= row
    strict = col < row
    eye = col == row
    incl_bf = incl.astype(BF16)
    same = lambda bits: (row >> bits) == (col >> bits)
    lw = load(lw_ref)
    cum = functools.reduce(lambda a, b: a + b, [_bmm(incl_bf, part) for part in _split3(lw)])
    e_neg = jnp.exp(-cum)
    g_last = jnp.exp(cum[:, c - 1:c, :])
    kc = load(kk_ref) * jnp.exp(cum - lw)
    rh = load(r_ref) * jnp.exp(cum)
    kh = load(k_ref) * e_neg
    bh = load(kka_ref) * e_neg
    v = load(v_ref)
    a_kb = jnp.where(strict, _bmm_nt(kc, bh), 0.0)
    a_kk = jnp.where(strict, _bmm_nt(kc, kh), 0.0)
    a_rb = jnp.where(incl, _bmm_nt(rh, bh), 0.0)
    a_rk = jnp.where(incl, _bmm_nt(rh, kh), 0.0)
    l1 = jnp.where(same(3), a_kb, 0.0)
    l2 = _bmm(l1, l1)
    l4 = _bmm(l2, l2)
    t = jnp.where(eye, 1.0, 0.0) - l1
    t = t + _bmm(t, l2)
    t = t + _bmm(t, l4)
    for bits in (4, 5, 6):
        off = jnp.where(same(bits) & jnp.logical_not(same(bits - 1)), a_kb, 0.0)
        t = t - _bmm(_bmm(t, off), t)
    p = _bmm(t, kc)
    w1 = _bmm(t, _bmm(a_kk, v))
    q_ref[0] = (rh - _bmm(a_rb, p)).reshape(hb, tb, hd)
    y1_ref[0] = (_bmm(a_rk, v) - _bmm(a_rb, w1)).reshape(hb, tb, hd)
    bt = bh * g_last
    m_ref[0] = (jnp.where(eye, g_last, 0.0) - _bmm_tn(bt, p)).reshape(hb, tb // c, hd, hd)
    n_ref[0] = (_bmm_tn(kh * g_last, v) - _bmm_tn(bt, w1)).reshape(hb, tb // c, hd, hd)


def _rwkv_inter_body(q_ref, y1_ref, m_ref, n_ref, h0_ref, y_ref, hout_ref, h_scr):
    c = RWKV_CHUNK
    tb = pl.program_id(1)

    @pl.when(tb == 0)
    def _():
        h_scr[...] = h0_ref[0]

    state = h_scr[...]
    for ci in range(q_ref.shape[2] // c):
        sl = pl.ds(ci * c, c)
        y_ref[0, :, sl, :] = _bmm(q_ref[0, :, sl, :], state) + y1_ref[0, :, sl, :]
        state = _bmm(m_ref[0, :, ci], state) + n_ref[0, :, ci]
    h_scr[...] = state

    @pl.when(tb == pl.num_programs(1) - 1)
    def _():
        hout_ref[0] = state


def _rwkv_chunked(r, lw, k, v, kk, kka, s0, heads_per_step=8, tokens_intra=256, tokens_inter=256):
    n, h, t, hd = r.shape
    c = RWKV_CHUNK
    nc = t // c
    rows = pl.BlockSpec((1, heads_per_step, tokens_intra, hd), lambda i, g, j: (i, g, j, 0))
    mats = pl.BlockSpec((1, heads_per_step, tokens_intra // c, hd, hd), lambda i, g, j: (i, g, j, 0, 0))
    rshape = jax.ShapeDtypeStruct((n, h, t, hd), F32)
    mshape = jax.ShapeDtypeStruct((n, h, nc, hd, hd), F32)
    q, y1, m, nn = pl.pallas_call(
        _rwkv_intra_body,
        grid=(n, h // heads_per_step, t // tokens_intra),
        in_specs=[rows] * 6,
        out_specs=[rows, rows, mats, mats],
        out_shape=[rshape, rshape, mshape, mshape],
        compiler_params=_params("parallel", "parallel", "parallel"),
        name="rwkv_intra",
    )(r, lw, k, v, kk, kka)
    rows = pl.BlockSpec((1, h, tokens_inter, hd), lambda i, j: (i, 0, j, 0))
    mats = pl.BlockSpec((1, h, tokens_inter // c, hd, hd), lambda i, j: (i, 0, j, 0, 0))
    state = pl.BlockSpec((1, h, hd, hd), lambda i, j: (i, 0, 0, 0))
    y, h_out = pl.pallas_call(
        _rwkv_inter_body,
        grid=(n, t // tokens_inter),
        in_specs=[rows, rows, mats, mats, state],
        out_specs=[rows, state],
        out_shape=[rshape, jax.ShapeDtypeStruct((n, h, hd, hd), F32)],
        scratch_shapes=[pltpu.VMEM((h, hd, hd), F32)],
        compiler_params=_params("parallel", "arbitrary"),
        name="rwkv_inter",
    )(q, y1, m, nn, jnp.swapaxes(s0, -1, -2))
    return y, jnp.swapaxes(h_out, -1, -2)


def _rwkv_scan_body(r_ref, lw_ref, k_ref, kk_ref, kka_ref, vt_ref, s0_ref, yt_ref, sout_ref, s_scr, y_scr,
                    *, n_tokens):
    tb = pl.program_id(1)

    @pl.when(tb == 0)
    def _():
        s_scr[...] = s0_ref[0]

    y_scr[...] = jnp.zeros_like(y_scr)
    lane = lax.broadcasted_iota(jnp.int32, (1, 1, SCAN_BLOCK), 2)

    def step(u, carry):
        onehot = lane == u
        row = lambda ref: ref[0, :, pl.ds(u, 1), :]
        s = s_scr[...]
        v_col = jnp.sum(jnp.where(onehot, vt_ref[0], 0.0), axis=-1, keepdims=True)
        sa = jnp.sum(s * row(kk_ref), axis=-1, keepdims=True)
        s = s * jnp.exp(row(lw_ref)) - sa * row(kka_ref) + v_col * row(k_ref)
        y = jnp.sum(s * row(r_ref), axis=-1, keepdims=True)
        s_scr[...] = s
        y_scr[...] = jnp.where(onehot, y, y_scr[...])
        return carry

    lax.fori_loop(0, jnp.minimum(SCAN_BLOCK, n_tokens - tb * SCAN_BLOCK), step, 0)
    yt_ref[0] = y_scr[...]

    @pl.when(tb == pl.num_programs(1) - 1)
    def _():
        sout_ref[0] = s_scr[...]


def _rwkv_scan(r, lw, k, kk, kka, v, s0, n_tokens):
    n, h, tp, hd = r.shape
    vt = jnp.swapaxes(v, -1, -2)
    rows = pl.BlockSpec((1, h, SCAN_BLOCK, hd), lambda i, t: (i, 0, t, 0))
    cols = pl.BlockSpec((1, h, hd, SCAN_BLOCK), lambda i, t: (i, 0, 0, t))
    state = pl.BlockSpec((1, h, hd, hd), lambda i, t: (i, 0, 0, 0))
    yt, s_out = pl.pallas_call(
        functools.partial(_rwkv_scan_body, n_tokens=n_tokens),
        grid=(n, tp // SCAN_BLOCK),
        in_specs=[rows] * 5 + [cols, state],
        out_specs=[cols, state],
        out_shape=[jax.ShapeDtypeStruct((n, h, hd, tp), F32), jax.ShapeDtypeStruct((n, h, hd, hd), F32)],
        scratch_shapes=[pltpu.VMEM((h, hd, hd), F32), pltpu.VMEM((h, hd, SCAN_BLOCK), F32)],
        compiler_params=_params("parallel", "arbitrary"),
        name="rwkv_scan",
    )(r, lw, k, kk, kka, vt, s0)
    return jnp.swapaxes(yt, -1, -2), s_out


def _rwkv_post_body(y_ref, r_ref, k_ref, v_ref, g_ref, rk_ref, gng_ref, gnb_ref, o_ref):
    outs = []
    for h in range(C_HEADS):
        sl = slice(h * HEAD_DIM, (h + 1) * HEAD_DIM)
        y = y_ref[0, h]
        d = y - jnp.mean(y, axis=-1, keepdims=True)
        var = jnp.mean(d * d, axis=-1, keepdims=True)
        yn = d * lax.rsqrt(var + GN_EPS) * gng_ref[:, sl] + gnb_ref[:, sl]
        bonus = jnp.sum(r_ref[0, h] * k_ref[0, h] * rk_ref[:, sl], axis=-1, keepdims=True) * v_ref[0, h]
        outs.append(yn + bonus)
    o_ref[0] = (jnp.concatenate(outs, axis=-1) * g_ref[0]).astype(o_ref.dtype)


def _rwkv_post(y, r, k, v, g, r_k, gn_g, gn_b, tm):
    n, h, t, hd = y.shape
    c = C_WIDTH
    head = pl.BlockSpec((1, h, tm, hd), lambda i, j: (i, 0, j, 0))
    nat = pl.BlockSpec((1, tm, c), lambda i, j: (i, j, 0))
    par = pl.BlockSpec((1, c), lambda i, j: (0, 0))
    return pl.pallas_call(
        _rwkv_post_body,
        grid=(n, t // tm),
        in_specs=[head] * 4 + [nat] + [par] * 3,
        out_specs=nat,
        out_shape=jax.ShapeDtypeStruct((n, t, c), BF16),
        compiler_params=_params("parallel", "parallel"),
        name="rwkv_post",
    )(y, r, k, v, g, r_k.reshape(1, c), gn_g.reshape(1, c), gn_b.reshape(1, c))


def _mem_attn_body(x_ref, wq_ref, mk_ref, mv_ref, wo_ref, g_ref, beta_ref, o_ref):
    x = x_ref[0]
    q = jnp.dot(x.astype(BF16), wq_ref[...], preferred_element_type=F32)
    scale = MEM_HEAD_DIM ** -0.5
    heads = []
    for h in range(MEM_HEADS):
        sl = slice(h * MEM_HEAD_DIM, (h + 1) * MEM_HEAD_DIM)
        s = _bdot_nt(q[:, sl], mk_ref[0][:, sl]) * scale
        m = jnp.max(s, axis=-1, keepdims=True)
        p = jnp.exp(s - m)
        p = p / jnp.sum(p, axis=-1, keepdims=True)
        heads.append(_bdot(p, mv_ref[0][:, sl]))
    o = jnp.concatenate(heads, axis=-1)
    h_out = jnp.dot(o.astype(BF16), wo_ref[...], preferred_element_type=F32)
    o_ref[0] = _layer_norm(DEEPNORM_ALPHA * x + h_out, g_ref[...], beta_ref[...])


def _memory_attention(x, mem_k, mem_v, w_q, w_o, g, beta, tm):
    n, t, d = x.shape
    mt = mem_k.shape[1]
    full = lambda a, b: pl.BlockSpec((a, b), lambda i, j: (0, 0))
    xspec = pl.BlockSpec((1, tm, d), lambda i, j: (i, j, 0))
    mem = pl.BlockSpec((1, mt, MEM_WIDTH), lambda i, j: (i, 0, 0))
    return pl.pallas_call(
        _mem_attn_body,
        grid=(n, t // tm),
        in_specs=[xspec, full(d, MEM_WIDTH), mem, mem, full(MEM_WIDTH, d), full(1, d), full(1, d)],
        out_specs=xspec,
        out_shape=jax.ShapeDtypeStruct((n, t, d), F32),
        compiler_params=_params("parallel", "parallel"),
        name="mem_attn",
    )(x, w_q, mem_k, mem_v, w_o, g.reshape(1, d), beta.reshape(1, d))


def _router_body(x_ref, w_ref, b_ref, idx_ref, gate_ref):
    x = x_ref[...]
    w = w_ref[...]
    xh = x.astype(BF16)
    xl = (x - xh.astype(F32)).astype(BF16)
    wh = w.astype(BF16)
    wl = (w - wh.astype(F32)).astype(BF16)
    d = lambda a, b: jnp.dot(a, b, preferred_element_type=F32)
    logits = d(xh, wh) + d(xh, wl) + d(xl, wh) + b_ref[...]
    lane = lax.broadcasted_iota(jnp.int32, logits.shape, 1).astype(F32)
    neg = -jnp.inf
    work = jnp.where(lane < N_EXPERTS, logits, neg)
    idx_out = jnp.zeros(logits.shape, F32)
    val_out = jnp.full(logits.shape, neg, F32)
    for k in range(TOP_K):
        m = jnp.max(work, axis=-1, keepdims=True)
        first = jnp.min(jnp.where(work == m, lane, float(LANES)), axis=-1, keepdims=True)
        idx_out = jnp.where(lane == k, first, idx_out)
        val_out = jnp.where(lane == k, m, val_out)
        work = jnp.where(lane == first, neg, work)
    e = jnp.exp(val_out - jnp.max(val_out, axis=-1, keepdims=True))
    idx_ref[...] = idx_out.astype(jnp.int32)
    gate_ref[...] = e / jnp.sum(e, axis=-1, keepdims=True)


def _router(x, w, b, tm):
    rows, d = x.shape
    wp = jnp.pad(w, ((0, 0), (0, LANES - N_EXPERTS)))
    bp = jnp.pad(b, (0, LANES - N_EXPERTS)).reshape(1, LANES)
    out = pl.BlockSpec((tm, LANES), lambda i: (i, 0))
    idx, gates = pl.pallas_call(
        _router_body,
        grid=(rows // tm,),
        in_specs=[pl.BlockSpec((tm, d), lambda i: (i, 0)),
                  pl.BlockSpec((d, LANES), lambda i: (0, 0)),
                  pl.BlockSpec((1, LANES), lambda i: (0, 0))],
        out_specs=[out, out],
        out_shape=[jax.ShapeDtypeStruct((rows, LANES), jnp.int32), jax.ShapeDtypeStruct((rows, LANES), F32)],
        compiler_params=_params("parallel"),
        name="router",
    )(x, wp, bp)
    return idx[:, :TOP_K], gates[:, :TOP_K]


def _expert_up_body(be_ref, x_ref, wg_ref, wu_ref, bg_ref, bu_ref, h_ref):
    x = x_ref[...].astype(BF16)
    gate = jnp.dot(x, wg_ref[0, 0].astype(BF16), preferred_element_type=F32) + bg_ref[0, 0]
    up = jnp.dot(x, wu_ref[0, 0].astype(BF16), preferred_element_type=F32) + bu_ref[0, 0]
    gate = jnp.minimum(gate, SWIGLU_LIMIT)
    up = jnp.clip(up, -SWIGLU_LIMIT, SWIGLU_LIMIT)
    h_ref[...] = ((up + 1.0) * gate * _sigmoid(SWIGLU_ALPHA * gate)).astype(h_ref.dtype)


def _expert_down_body(be_ref, h_ref, w_ref, b_ref, y_ref):
    y_ref[...] = jnp.dot(h_ref[...], w_ref[0, 0].astype(BF16), preferred_element_type=F32) + b_ref[0, 0]


def _expert_ffn(rows, block_e, layer, w_gate_up, b_gate_up, w_down, b_down, tn):
    nr, d = rows.shape
    nb = nr // MOE_BLOCK
    nj = D_FF // tn
    b_gu = b_gate_up.reshape(DEPTH, N_EXPERTS, 1, 2 * D_FF)
    b_dn = b_down.reshape(DEPTH, N_EXPERTS, 1, d)
    hid = pl.pallas_call(
        _expert_up_body,
        grid_spec=pltpu.PrefetchScalarGridSpec(
            num_scalar_prefetch=1,
            grid=(nj, nb),
            in_specs=[pl.BlockSpec((MOE_BLOCK, d), lambda j, i, be: (i, 0)),
                      pl.BlockSpec((1, 1, d, tn), lambda j, i, be: (layer, be[i], 0, j)),
                      pl.BlockSpec((1, 1, d, tn), lambda j, i, be: (layer, be[i], 0, j + nj)),
                      pl.BlockSpec((1, 1, 1, tn), lambda j, i, be: (layer, be[i], 0, j)),
                      pl.BlockSpec((1, 1, 1, tn), lambda j, i, be: (layer, be[i], 0, j + nj))],
            out_specs=pl.BlockSpec((MOE_BLOCK, tn), lambda j, i, be: (i, j))),
        out_shape=jax.ShapeDtypeStruct((nr, D_FF), BF16),
        compiler_params=_params("parallel", "arbitrary"),
        name="expert_up",
    )(block_e, rows, w_gate_up, w_gate_up, b_gu, b_gu)
    nk = d // tn
    return pl.pallas_call(
        _expert_down_body,
        grid_spec=pltpu.PrefetchScalarGridSpec(
            num_scalar_prefetch=1,
            grid=(nk, nb),
            in_specs=[pl.BlockSpec((MOE_BLOCK, D_FF), lambda j, i, be: (i, 0)),
                      pl.BlockSpec((1, 1, D_FF, tn), lambda j, i, be: (layer, be[i], 0, j)),
                      pl.BlockSpec((1, 1, 1, tn), lambda j, i, be: (layer, be[i], 0, j))],
            out_specs=pl.BlockSpec((MOE_BLOCK, tn), lambda j, i, be: (i, j))),
        out_shape=jax.ShapeDtypeStruct((nr, d), F32),
        compiler_params=_params("parallel", "arbitrary"),
        name="expert_down",
    )(block_e, hid, w_down, b_dn)


def _moe_combine_body(x_ref, y_ref, gate_ref, g_ref, beta_ref, o_ref):
    gates = gate_ref[...]
    acc = gates[:, 0:1] * y_ref[0]
    for k in range(1, TOP_K):
        acc = acc + gates[:, k:k + 1] * y_ref[k]
    o_ref[...] = _layer_norm(DEEPNORM_ALPHA * x_ref[...] + acc, g_ref[...], beta_ref[...])


def _moe_combine(x, y_k, gates, g, beta, tm):
    rows, d = x.shape
    gp = jnp.pad(gates, ((0, 0), (0, LANES - TOP_K)))
    xs = pl.BlockSpec((tm, d), lambda i: (i, 0))
    par = pl.BlockSpec((1, d), lambda i: (0, 0))
    return pl.pallas_call(
        _moe_combine_body,
        grid=(rows // tm,),
        in_specs=[xs, pl.BlockSpec((TOP_K, tm, d), lambda i: (0, i, 0)),
                  pl.BlockSpec((tm, LANES), lambda i: (i, 0)), par, par],
        out_specs=xs,
        out_shape=jax.ShapeDtypeStruct((rows, d), F32),
        compiler_params=_params("parallel"),
        name="moe_combine",
    )(x, y_k, gp, g.reshape(1, d), beta.reshape(1, d))


def _moe_layer(x, layer, router_w, router_b, w_gate_up, b_gate_up, w_down, b_down, g, beta):
    rows, d = x.shape
    top_idx, gates = _router(x, router_w[layer], router_b[layer], 256)
    n_assign = rows * TOP_K
    flat_e = top_idx.reshape(-1)
    order = jnp.argsort(flat_e)
    e_sorted = flat_e[order]
    counts = jnp.zeros((N_EXPERTS,), jnp.int32).at[flat_e].add(1)
    padded = (counts + MOE_BLOCK - 1) // MOE_BLOCK * MOE_BLOCK
    pad_end = jnp.cumsum(padded)
    pad_start = pad_end - padded
    start = jnp.cumsum(counts) - counts
    dest = pad_start[e_sorted] + jnp.arange(n_assign, dtype=jnp.int32) - start[e_sorted]
    n_blocks = -(-n_assign // MOE_BLOCK) + N_EXPERTS
    row_tok = jnp.zeros((n_blocks * MOE_BLOCK,), jnp.int32).at[dest].set((order // TOP_K).astype(jnp.int32))
    block_e = jnp.minimum(jnp.searchsorted(pad_end, jnp.arange(n_blocks, dtype=jnp.int32) * MOE_BLOCK,
                                           side='right'), N_EXPERTS - 1).astype(jnp.int32)
    pos = jnp.zeros((n_assign,), jnp.int32).at[order].set(dest.astype(jnp.int32)).reshape(rows, TOP_K)
    y_rows = _expert_ffn(x[row_tok], block_e, layer, w_gate_up, b_gate_up, w_down, b_down, 512)
    return _moe_combine(x, y_rows[pos.T], gates, g, beta, 256)


def kernel(x_prompt, x_sample, cache_a_k, cache_a_v, cache_b_k, cache_b_v, state_c_wkv, state_c_shift,
           cache_mem_k, cache_mem_v, mem_prompt, w_in, a_sink, c_mu, c_w0, c_w_up, c_a0, c_a_up, c_g_up,
           c_k_k, c_k_a, c_r_k, c_gn_g, c_gn_b, w_out, ln_g, ln_b, w_mem_q, w_mem_k, w_mem_v, w_mem_o,
           router_w, router_b, w_gate_up, b_gate_up, w_down, b_down):
    n_p, t_p, d = x_prompt.shape
    n_s, t_s, _ = x_sample.shape
    la = cache_a_k.shape[2]
    lb = cache_b_k.shape[2]
    assert t_s == 1 and t_p % B_MAX_WINDOW == 0 and la == A_WINDOW and lb == B_MAX_WINDOW
    rows_p = n_p * t_p
    rows_all = _round_up(rows_p + n_s, 256)
    xp = x_prompt.reshape(rows_p, d)
    xs = x_sample.reshape(n_s, d)
    mem_bf = mem_prompt.reshape(n_p * MEM_TOKENS, d).astype(BF16)
    branches_b = tuple((lb - w, dil) for w, dil in B_BRANCHES)
    no_sink = jnp.full((1, B_HEADS, HEAD_DIM), -jnp.inf, F32)
    zero_state = jnp.zeros((n_p, C_HEADS, HEAD_DIM, HEAD_DIM), F32)
    states = []
    for l in range(DEPTH):
        cpar = (c_mu[l], c_w0[l], c_w_up[l], c_a0[l], c_a_up[l], c_g_up[l], c_k_k[l], c_k_a[l])
        w_in_bf = w_in[l].astype(BF16)
        w_ab, w_fc = w_in_bf[:, :AB_COLS], w_in_bf[:, AB_COLS:]
        w_out_bf = w_out[l].astype(BF16)
        w_q_bf = w_mem_q[l].astype(BF16)
        w_o_bf = w_mem_o[l].astype(BF16)
        r_k = c_r_k[l].reshape(-1)

        xp_bf = xp.astype(BF16)
        qkv = _matmul(xp_bf, w_ab, 512, AB_COLS // 3).reshape(n_p, t_p, AB_COLS)
        fc = _matmul(xp_bf, w_fc, 512, C_SHIFT_W // 2).reshape(n_p, t_p, C_SHIFT_W)
        oa = _swa_attention(qkv, a_sink[l])
        ob = _dilated_attention(qkv)
        r, lw, k, v, kk, kka, g = _rwkv_pre(fc, None, *cpar, 128)
        y, p_wkv = _rwkv_chunked(r, lw, k, v, kk, kka, zero_state)
        oc = _rwkv_post(y, r, k, v, g, r_k, c_gn_g[l], c_gn_b[l], 128)
        xp = _out_proj_ln(oa.reshape(rows_p, -1), ob.reshape(rows_p, -1), oc.reshape(rows_p, -1), w_out_bf, xp,
                          ln_g[l, 0], ln_b[l, 0], 256)
        mk = _matmul(mem_bf, w_mem_k[l].astype(BF16), 512, 512).reshape(n_p, MEM_TOKENS, MEM_WIDTH)
        mv = _matmul(mem_bf, w_mem_v[l].astype(BF16), 512, 512).reshape(n_p, MEM_TOKENS, MEM_WIDTH)
        xp = _memory_attention(xp.reshape(n_p, t_p, d), mk, mv, w_q_bf, w_o_bf,
                               ln_g[l, 1], ln_b[l, 1], 256).reshape(rows_p, d)
        ka = qkv[:, t_p - A_WINDOW:, A_HEADS * HEAD_DIM:A_HEADS * HEAD_DIM + A_KV_HEADS * HEAD_DIM]
        va = qkv[:, t_p - A_WINDOW:, A_HEADS * HEAD_DIM + A_KV_HEADS * HEAD_DIM:A_COLS]
        kb = qkv[:, t_p - B_MAX_WINDOW:, A_COLS + B_HEADS * HEAD_DIM:A_COLS + 2 * B_HEADS * HEAD_DIM]
        vb = qkv[:, t_p - B_MAX_WINDOW:, A_COLS + 2 * B_HEADS * HEAD_DIM:]
        p_items = (ka.reshape(n_p, A_WINDOW, A_KV_HEADS, HEAD_DIM), va.reshape(n_p, A_WINDOW, A_KV_HEADS, HEAD_DIM),
                   kb.reshape(n_p, B_MAX_WINDOW, B_HEADS, HEAD_DIM), vb.reshape(n_p, B_MAX_WINDOW, B_HEADS, HEAD_DIM),
                   p_wkv, fc[:, -1],
                   mk.reshape(n_p, MEM_TOKENS, MEM_HEADS, MEM_HEAD_DIM),
                   mv.reshape(n_p, MEM_TOKENS, MEM_HEADS, MEM_HEAD_DIM))

        proj = _matmul(xs.astype(BF16), w_in_bf, 32, 512)
        qkv_s, fc_s = proj[:, :AB_COLS], proj[:, AB_COLS:]
        c0 = A_HEADS * HEAD_DIM
        c1 = c0 + A_KV_HEADS * HEAD_DIM
        ka_s = qkv_s[:, c0:c1].reshape(n_s, A_KV_HEADS, HEAD_DIM)
        va_s = qkv_s[:, c1:A_COLS].reshape(n_s, A_KV_HEADS, HEAD_DIM)
        b0 = A_COLS + B_HEADS * HEAD_DIM
        b1 = b0 + B_HEADS * HEAD_DIM
        kb_s = qkv_s[:, b0:b1].reshape(n_s, B_HEADS, HEAD_DIM)
        vb_s = qkv_s[:, b1:].reshape(n_s, B_HEADS, HEAD_DIM)
        q_g = qkv_s[:, :c0].reshape(n_s, A_KV_HEADS, A_GROUP, HEAD_DIM).transpose(0, 2, 1, 3)
        sink_g = jnp.broadcast_to(a_sink[l].reshape(A_KV_HEADS, A_GROUP).T[:, :, None],
                                  (A_GROUP, A_KV_HEADS, HEAD_DIM))
        oa_s = _cache_attention(q_g, cache_a_k, cache_a_v, l, ka_s, va_s, sink_g, ((0, 1),))
        oa_s = oa_s.transpose(0, 2, 1, 3).reshape(n_s, -1)
        ob_s = _cache_attention(qkv_s[:, A_COLS:b0].reshape(n_s, 1, B_HEADS, HEAD_DIM), cache_b_k, cache_b_v, l,
                                kb_s, vb_s, no_sink, branches_b).reshape(n_s, -1)
        r, lw, k, v, kk, kka, g = _rwkv_pre(fc_s[None], state_c_shift[l][None], *cpar, n_s)
        seq = lambda z: jnp.pad(z[0].transpose(1, 0, 2)[:, :, None], ((0, 0), (0, 0), (0, SCAN_BLOCK - 1), (0, 0)))
        y_s, s_wkv = _rwkv_scan(seq(r), seq(lw), seq(k), seq(kk), seq(kka), seq(v), state_c_wkv[l], 1)
        oc_s = _rwkv_post(y_s[:, :, 0].transpose(1, 0, 2)[None], r, k, v, g, r_k, c_gn_g[l], c_gn_b[l], n_s)
        xs = _out_proj_ln(oa_s.astype(BF16), ob_s.astype(BF16), oc_s[0], w_out_bf, xs, ln_g[l, 0], ln_b[l, 0], n_s)
        xs_pad = jnp.pad(xs[:, None], ((0, 0), (0, SUBLANES - 1), (0, 0)))
        xs = _memory_attention(xs_pad, cache_mem_k[l].reshape(n_s, MEM_TOKENS, MEM_WIDTH),
                               cache_mem_v[l].reshape(n_s, MEM_TOKENS, MEM_WIDTH), w_q_bf, w_o_bf,
                               ln_g[l, 1], ln_b[l, 1], SUBLANES)[:, 0]
        s_items = (ka_s[:, None], va_s[:, None], kb_s[:, None], vb_s[:, None], s_wkv, fc_s)

        x_all = jnp.concatenate([xp, xs, jnp.zeros((rows_all - rows_p - n_s, d), F32)], axis=0)
        x_all = _moe_layer(x_all, l, router_w, router_b, w_gate_up, b_gate_up, w_down, b_down,
                           ln_g[l, 2], ln_b[l, 2])
        xp, xs = x_all[:rows_p], x_all[rows_p:rows_p + n_s]
        states.append(p_items + s_items)

    stacked = [jnp.stack(z, axis=0) for z in zip(*states)]
    return (xp.reshape(n_p, t_p, d), xs.reshape(n_s, t_s, d), *stacked)
```

```python
import functools
import math

import jax
import jax.numpy as jnp
from jax import lax
from jax.experimental import pallas as pl
from jax.experimental.pallas import tpu as pltpu

F32 = jnp.float32
BF16 = jnp.bfloat16

D_MODEL = 2048
DEPTH = 4
HEAD_DIM = 64
A_HEADS = 8
A_KV_HEADS = 2
A_GROUP = 4
A_WINDOW = 128
B_HEADS = 8
B_BRANCHES = ((128, 1), (512, 4), (2048, 16))
B_MAX_WINDOW = 2048
C_HEADS = 16
C_WIDTH = C_HEADS * HEAD_DIM
C_LORA_W = 64
C_LORA_A = 64
C_LORA_G = 128
C_SHIFT_W = 3 * C_WIDTH + C_LORA_W + C_LORA_A + C_LORA_G
A_COLS = (A_HEADS + 2 * A_KV_HEADS) * HEAD_DIM
B_COLS = 3 * B_HEADS * HEAD_DIM
AB_COLS = A_COLS + B_COLS
MEM_TOKENS = 256
MEM_HEADS = 4
MEM_HEAD_DIM = 128
MEM_WIDTH = MEM_HEADS * MEM_HEAD_DIM
N_EXPERTS = 32
TOP_K = 4
D_FF = D_MODEL
SWIGLU_ALPHA = 1.702
SWIGLU_LIMIT = 7.0
BAND = 128
LN_EPS = 1e-5
GN_EPS = 64e-5
DECAY_SCALE = math.exp(-0.5)
DEEPNORM_ALPHA = (2 * DEPTH) ** 0.25

LANES = 128
SUBLANES = 8
VMEM_LIMIT = 48 * 1024 * 1024
SCAN_BLOCK = 128
RWKV_CHUNK = 64
MOE_BLOCK = 512
PAIR = LANES // HEAD_DIM


def _params(*sem):
    return pltpu.CompilerParams(dimension_semantics=sem, vmem_limit_bytes=VMEM_LIMIT)


def _round_up(x, m):
    return (x + m - 1) // m * m


def _sigmoid(x):
    return 1.0 / (1.0 + jnp.exp(-x))


def _layer_norm(z, g, b):
    mu = jnp.mean(z, axis=-1, keepdims=True)
    d = z - mu
    var = jnp.mean(d * d, axis=-1, keepdims=True)
    return d * lax.rsqrt(var + LN_EPS) * g + b


def _split3(x):
    h = x.astype(BF16)
    r = x - h.astype(F32)
    m = r.astype(BF16)
    l = (r - m.astype(F32)).astype(BF16)
    return h, m, l


def _bdot(a, b):
    return jnp.dot(a.astype(BF16), b.astype(BF16), preferred_element_type=F32)


def _bdot_nt(a, b):
    return lax.dot_general(a.astype(BF16), b.astype(BF16), (((1,), (1,)), ((), ())), preferred_element_type=F32)


def _bmm(a, b):
    return lax.dot_general(a.astype(BF16), b.astype(BF16), (((2,), (1,)), ((0,), (0,))), preferred_element_type=F32)


def _bmm_nt(a, b):
    return lax.dot_general(a.astype(BF16), b.astype(BF16), (((2,), (2,)), ((0,), (0,))), preferred_element_type=F32)


def _bmm_tn(a, b):
    return lax.dot_general(a.astype(BF16), b.astype(BF16), (((1,), (1,)), ((0,), (0,))), preferred_element_type=F32)


def _mm_body(a_ref, b_ref, o_ref):
    o_ref[...] = jnp.dot(a_ref[...], b_ref[...], preferred_element_type=F32)


def _matmul(a, b, tm, tn):
    m, k = a.shape
    n = b.shape[1]
    mp = _round_up(m, tm)
    if mp != m:
        a = jnp.pad(a, ((0, mp - m), (0, 0)))
    out = pl.pallas_call(
        _mm_body,
        grid=(n // tn, mp // tm),
        in_specs=[pl.BlockSpec((tm, k), lambda j, i: (i, 0)),
                  pl.BlockSpec((k, tn), lambda j, i: (0, j))],
        out_specs=pl.BlockSpec((tm, tn), lambda j, i: (i, j)),
        out_shape=jax.ShapeDtypeStruct((mp, n), F32),
        compiler_params=_params("parallel", "parallel"),
        name="mm",
    )(a, b)
    return out[:m]


def _out_proj_body(oa_ref, ob_ref, oc_ref, wa_ref, wb_ref, wc_ref, res_ref, g_ref, beta_ref, o_ref):
    dot = lambda a, w: jnp.dot(a[...], w[...], preferred_element_type=F32)
    h = dot(oa_ref, wa_ref) + dot(ob_ref, wb_ref) + dot(oc_ref, wc_ref)
    o_ref[...] = _layer_norm(DEEPNORM_ALPHA * res_ref[...] + h, g_ref[...], beta_ref[...])


def _out_proj_ln(oa, ob, oc, w, res, g, beta, tm):
    m = oa.shape[0]
    d = w.shape[1]
    ka, kc = oa.shape[1], oc.shape[1]
    row = lambda width: pl.BlockSpec((tm, width), lambda i: (i, 0))
    par = pl.BlockSpec((1, d), lambda i: (0, 0))
    return pl.pallas_call(
        _out_proj_body,
        grid=(m // tm,),
        in_specs=[row(ka), row(ka), row(kc),
                  pl.BlockSpec((ka, d), lambda i: (0, 0)), pl.BlockSpec((ka, d), lambda i: (1, 0)),
                  pl.BlockSpec((kc, d), lambda i: (1, 0)), row(d), par, par],
        out_specs=row(d),
        out_shape=jax.ShapeDtypeStruct((m, d), F32),
        compiler_params=_params("parallel"),
        name="out_proj_ln",
    )(oa, ob, oc, w, w, w, res, g.reshape(1, d), beta.reshape(1, d))


def _band_bias():
    row = lax.broadcasted_iota(jnp.int32, (BAND, 2 * BAND), 0)
    col = lax.broadcasted_iota(jnp.int32, (BAND, 2 * BAND), 1)
    dist = jnp.where(col < BAND, BAND + row - col, row - (col - BAND))
    return jnp.where((dist >= 0) & (dist <= BAND), 0.0, -jnp.inf), col < BAND


def _softmax_pv(s, v_bf, sink=None):
    m = jnp.max(s, axis=-1, keepdims=True)
    if sink is not None:
        m = jnp.maximum(m, sink)
    p = jnp.exp(s - m)
    denom = jnp.sum(p, axis=-1, keepdims=True)
    if sink is not None:
        denom = denom + jnp.exp(sink - m)
    num = jnp.dot(p.astype(BF16), v_bf, preferred_element_type=F32)
    return num / denom, m + jnp.log(denom)


def _swa_body(sink_ref, q_ref, kp_ref, kc_ref, vp_ref, vc_ref, o_ref):
    b = pl.program_id(1)
    scale = HEAD_DIM ** -0.5
    bias, is_prev = _band_bias()
    bias = bias + jnp.where(is_prev, jnp.where(b > 0, 0.0, -jnp.inf), 0.0)
    q = q_ref[0]
    outs = []
    for kv in range(A_KV_HEADS):
        sl = slice(kv * HEAD_DIM, (kv + 1) * HEAD_DIM)
        k_cat = jnp.concatenate([kp_ref[0][:, sl], kc_ref[0][:, sl]], axis=0).astype(BF16)
        v_cat = jnp.concatenate([vp_ref[0][:, sl], vc_ref[0][:, sl]], axis=0).astype(BF16)
        for g in range(A_GROUP):
            h = kv * A_GROUP + g
            s = _bdot_nt(q[:, h * HEAD_DIM:(h + 1) * HEAD_DIM], k_cat) * scale + bias
            o, _ = _softmax_pv(s, v_cat, sink_ref[h])
            outs.append(o)
    o_ref[0] = jnp.concatenate(outs, axis=-1).astype(o_ref.dtype)


def _swa_attention(qkv, sink):
    n, t, _ = qkv.shape
    qw = A_HEADS * HEAD_DIM
    kblk = qw // LANES
    prev = lambda c: pl.BlockSpec((1, BAND, LANES), lambda i, b: (i, jnp.maximum(b - 1, 0), c))
    cur = lambda c: pl.BlockSpec((1, BAND, LANES), lambda i, b: (i, b, c))
    qspec = pl.BlockSpec((1, BAND, qw), lambda i, b: (i, b, 0))
    return pl.pallas_call(
        _swa_body,
        grid=(n, t // BAND),
        in_specs=[pl.BlockSpec(memory_space=pltpu.SMEM), qspec, prev(kblk), cur(kblk), prev(kblk + 1), cur(kblk + 1)],
        out_specs=qspec,
        out_shape=jax.ShapeDtypeStruct((n, t, qw), BF16),
        compiler_params=_params("parallel", "parallel"),
        name="swa_attn",
    )(sink.astype(F32), qkv, qkv, qkv, qkv, qkv)


def _dilated_body(q_ref, kp_ref, kc_ref, vp_ref, vc_ref, o_ref, kcat, vcat, ob_scr, lse_scr):
    span = B_MAX_WINDOW
    sb = pl.program_id(2)
    scale = HEAD_DIM ** -0.5
    kcat[0:span] = kp_ref[0]
    kcat[span:2 * span] = kc_ref[0]
    vcat[0:span] = vp_ref[0]
    vcat[span:2 * span] = vc_ref[0]
    bias0, is_prev = _band_bias()
    no_prev_span = jnp.where(sb > 0, 0.0, -jnp.inf)
    first_head = lax.broadcasted_iota(jnp.int32, (BAND, LANES), 1) < HEAD_DIM
    for bi, (window, dil) in enumerate(B_BRANCHES):
        nsub = span // dil // BAND

        def sub(i, carry, bi=bi, dil=dil, nsub=nsub):
            r = i // nsub
            j = i % nsub
            q_start = r + dil * BAND * j
            k_start = span + q_start - dil * BAND
            if dil == 1:
                q_rows, k_rows = pl.ds(q_start, BAND), pl.ds(k_start, 2 * BAND)
            else:
                q_rows = pl.ds(q_start, BAND, stride=dil)
                k_rows = pl.ds(k_start, 2 * BAND, stride=dil)
            q2 = q_ref.at[0][q_rows, :]
            k2 = kcat[k_rows, :]
            v2 = vcat[k_rows, :]
            bias = bias0 + jnp.where(is_prev, jnp.where(j == 0, no_prev_span, 0.0), 0.0)
            k_bf = k2.astype(BF16)
            v_bf = v2.astype(BF16)
            o_a, lse_a = _softmax_pv(_bdot_nt(jnp.where(first_head, q2, 0.0), k_bf) * scale + bias, v_bf)
            o_b, lse_b = _softmax_pv(_bdot_nt(jnp.where(first_head, 0.0, q2), k_bf) * scale + bias, v_bf)
            ob_scr.at[bi][q_rows, :] = jnp.where(first_head, o_a, o_b)
            lse_scr.at[bi][q_rows, :] = jnp.where(first_head, lse_a, lse_b)
            return carry

        lax.fori_loop(0, span // BAND, sub, 0, unroll=4)
    l0, l1, l2 = lse_scr[0], lse_scr[1], lse_scr[2]
    m = jnp.maximum(jnp.maximum(l0, l1), l2)
    e0, e1, e2 = jnp.exp(l0 - m), jnp.exp(l1 - m), jnp.exp(l2 - m)
    tot = e0 + e1 + e2
    o_ref[0] = ((e0 / tot) * ob_scr[0] + (e1 / tot) * ob_scr[1] + (e2 / tot) * ob_scr[2]).astype(o_ref.dtype)


def _dilated_attention(qkv):
    n, t, _ = qkv.shape
    span = B_MAX_WINDOW
    width = B_HEADS * HEAD_DIM
    q0 = A_COLS // LANES
    nhp = width // LANES
    blk = lambda c0, prev: pl.BlockSpec(
        (1, span, LANES),
        (lambda i, p, s: (i, jnp.maximum(s - 1, 0), c0 + p)) if prev else (lambda i, p, s: (i, s, c0 + p)))
    return pl.pallas_call(
        _dilated_body,
        grid=(n, nhp, t // span),
        in_specs=[blk(q0, False), blk(q0 + nhp, True), blk(q0 + nhp, False),
                  blk(q0 + 2 * nhp, True), blk(q0 + 2 * nhp, False)],
        out_specs=pl.BlockSpec((1, span, LANES), lambda i, p, s: (i, s, p)),
        out_shape=jax.ShapeDtypeStruct((n, t, width), BF16),
        scratch_shapes=[pltpu.VMEM((2 * span, LANES), F32), pltpu.VMEM((2 * span, LANES), F32),
                        pltpu.VMEM((len(B_BRANCHES), span, LANES), F32),
                        pltpu.VMEM((len(B_BRANCHES), span, LANES), F32)],
        compiler_params=_params("parallel", "parallel", "parallel"),
        name="dilated_attn",
    )(qkv, qkv, qkv, qkv, qkv)


def _cache_attn_body(q_ref, kc_ref, vc_ref, kn_ref, vn_ref, sink_ref, o_ref, *, branches, group):
    scale = HEAD_DIM ** -0.5
    k_new = kn_ref[0]
    v_new = vn_ref[0]
    for g in range(group):
        q = q_ref[0, g]
        sink = sink_ref[g]
        s_n = jnp.sum(k_new * q, axis=-1, keepdims=True) * scale
        outs, lses = [], []
        for start, stride in branches:
            rows = pl.ds(start, BAND) if stride == 1 else pl.ds(start, BAND, stride=stride)
            k_c = kc_ref.at[0, 0][rows]
            v_c = vc_ref.at[0, 0][rows]
            s_c = jnp.sum(k_c * q[None], axis=-1, keepdims=True) * scale
            m = jnp.maximum(jnp.maximum(jnp.max(s_c, axis=0), s_n), sink)
            p_c = jnp.exp(s_c - m[None])
            p_n = jnp.exp(s_n - m)
            denom = jnp.sum(p_c, axis=0) + p_n + jnp.exp(sink - m)
            num = jnp.sum(p_c * v_c, axis=0) + p_n * v_new
            outs.append(num / denom)
            lses.append(m + jnp.log(denom))
        if len(branches) == 1:
            o = outs[0]
        else:
            m = functools.reduce(jnp.maximum, lses)
            es = [jnp.exp(l - m) for l in lses]
            tot = functools.reduce(lambda a, b: a + b, es)
            o = functools.reduce(lambda a, b: a + b, [(e / tot) * o_b for e, o_b in zip(es, outs)])
        o_ref[0, g] = o


def _cache_attention(q, k_cache, v_cache, layer, k_new, v_new, sink, branches):
    n, group, hk, hd = q.shape
    lc = k_cache.shape[2]
    cache = pl.BlockSpec((1, 1, lc, hk, hd), lambda i: (layer, i, 0, 0, 0), pipeline_mode=pl.Buffered(1))
    new = pl.BlockSpec((1, hk, hd), lambda i: (i, 0, 0))
    qspec = pl.BlockSpec((1, group, hk, hd), lambda i: (i, 0, 0, 0))
    return pl.pallas_call(
        functools.partial(_cache_attn_body, branches=branches, group=group),
        grid=(n,),
        in_specs=[qspec, cache, cache, new, new, pl.BlockSpec((group, hk, hd), lambda i: (0, 0, 0))],
        out_specs=qspec,
        out_shape=jax.ShapeDtypeStruct((n, group, hk, hd), F32),
        compiler_params=_params("parallel"),
        name="cache_attn",
    )(q, k_cache, v_cache, k_new, v_new, sink)


def _rwkv_pre_body(fc_ref, prev_ref, mu_ref, w0_ref, wup_ref, a0_ref, aup_ref, gup_ref, kk_w_ref, ka_w_ref,
                   r_ref, lw_ref, k_ref, v_ref, kk_ref, kka_ref, g_ref, *, shift_in_kernel):
    fc = fc_ref[0]
    if shift_in_kernel:
        last = jnp.where(pl.program_id(1) > 0, prev_ref[0][SUBLANES - 1:SUBLANES, :], 0.0)
        rowid = lax.broadcasted_iota(jnp.int32, fc.shape, 0)
        shifted = jnp.where(rowid == 0, last, pltpu.roll(fc, 1, 0))
    else:
        shifted = prev_ref[0]
    f = fc + (shifted - fc) * mu_ref[...]
    c = C_WIDTH
    r = f[:, 0:c]
    k = f[:, c:2 * c]
    v = f[:, 2 * c:3 * c]
    wd = f[:, 3 * c:3 * c + C_LORA_W]
    ad = f[:, 3 * c + C_LORA_W:3 * c + C_LORA_W + C_LORA_A]
    gd = f[:, 3 * c + C_LORA_W + C_LORA_A:]
    dot = lambda x, w_ref_: jnp.dot(x.astype(BF16), w_ref_[...], preferred_element_type=F32)
    log_decay = -DECAY_SCALE * _sigmoid(w0_ref[...] + dot(jnp.tanh(wd), wup_ref))
    a = _sigmoid(a0_ref[...] + dot(ad, aup_ref))
    g_ref[0] = dot(_sigmoid(gd), gup_ref)
    kk = k * kk_w_ref[...]
    k2 = k * (1.0 + (a - 1.0) * ka_w_ref[...])
    for h in range(C_HEADS):
        sl = slice(h * HEAD_DIM, (h + 1) * HEAD_DIM)
        kk_h = kk[:, sl]
        kk_h = kk_h / jnp.maximum(jnp.sqrt(jnp.sum(kk_h * kk_h, axis=-1, keepdims=True)), 1e-12)
        r_ref[0, h] = r[:, sl]
        lw_ref[0, h] = log_decay[:, sl]
        k_ref[0, h] = k2[:, sl]
        v_ref[0, h] = v[:, sl]
        kk_ref[0, h] = kk_h
        kka_ref[0, h] = kk_h * a[:, sl]


def _rwkv_pre(fc, shifted, mu, w0, w_up, a0, a_up, g_up, k_k, k_a, tm):
    n, t, _ = fc.shape
    c = C_WIDTH
    in_kernel = shifted is None
    fspec = pl.BlockSpec((1, tm, C_SHIFT_W), lambda i, j: (i, j, 0))
    if in_kernel:
        per = tm // SUBLANES
        pspec = pl.BlockSpec((1, SUBLANES, C_SHIFT_W), lambda i, j: (i, jnp.maximum(j * per - 1, 0), 0))
        shifted = fc
    else:
        pspec = fspec
    full = lambda a, b: pl.BlockSpec((a, b), lambda i, j: (0, 0))
    head = pl.BlockSpec((1, C_HEADS, tm, HEAD_DIM), lambda i, j: (i, 0, j, 0))
    hshape = jax.ShapeDtypeStruct((n, C_HEADS, t, HEAD_DIM), F32)
    return pl.pallas_call(
        functools.partial(_rwkv_pre_body, shift_in_kernel=in_kernel),
        grid=(n, t // tm),
        in_specs=[fspec, pspec, full(1, C_SHIFT_W), full(1, c), full(C_LORA_W, c),
                  full(1, c), full(C_LORA_A, c), full(C_LORA_G, c), full(1, c), full(1, c)],
        out_specs=[head] * 6 + [pl.BlockSpec((1, tm, c), lambda i, j: (i, j, 0))],
        out_shape=[hshape] * 6 + [jax.ShapeDtypeStruct((n, t, c), F32)],
        compiler_params=_params("parallel", "parallel"),
        name="rwkv_pre",
    )(fc, shifted, mu.reshape(1, -1), w0.reshape(1, c), w_up.astype(BF16), a0.reshape(1, c),
      a_up.astype(BF16), g_up.astype(BF16), k_k.reshape(1, c), k_a.reshape(1, c))


def _rwkv_intra_body(r_ref, lw_ref, k_ref, v_ref, kk_ref, kka_ref, q_ref, y1_ref, m_ref, n_ref):
    c = RWKV_CHUNK
    _, hb, tb, hd = r_ref.shape
    nb = hb * (tb // c)
    load = lambda ref: ref[0].reshape(nb, c, hd)
    row = lax.broadcasted_iota(jnp.int32, (nb, c, c), 1)
    col = lax.broadcasted_iota(jnp.int32, (nb, c, c), 2)
    incl = col <= row
    strict = col < row
    eye = col == row
    incl_bf = incl.astype(BF16)
    same = lambda bits: (row >> bits) == (col >> bits)
    lw = load(lw_ref)
    cum = functools.reduce(lambda a, b: a + b, [_bmm(incl_bf, part) for part in _split3(lw)])
    e_neg = jnp.exp(-cum)
    g_last = jnp.exp(cum[:, c - 1:c, :])
    kc = load(kk_ref) * jnp.exp(cum - lw)
    rh = load(r_ref) * jnp.exp(cum)
    kh = load(k_ref) * e_neg
    bh = load(kka_ref) * e_neg
    v = load(v_ref)
    a_kb = jnp.where(strict, _bmm_nt(kc, bh), 0.0)
    a_kk = jnp.where(strict, _bmm_nt(kc, kh), 0.0)
    a_rb = jnp.where(incl, _bmm_nt(rh, bh), 0.0)
    a_rk = jnp.where(incl, _bmm_nt(rh, kh), 0.0)
    l1 = jnp.where(same(3), a_kb, 0.0)
    l2 = _bmm(l1, l1)
    l4 = _bmm(l2, l2)
    t = jnp.where(eye, 1.0, 0.0) - l1
    t = t + _bmm(t, l2)
    t = t + _bmm(t, l4)
    for bits in (4, 5, 6):
        off = jnp.where(same(bits) & jnp.logical_not(same(bits - 1)), a_kb, 0.0)
        t = t - _bmm(_bmm(t, off), t)
    p = _bmm(t, kc)
    w1 = _bmm(t, _bmm(a_kk, v))
    q_ref[0] = (rh - _bmm(a_rb, p)).reshape(hb, tb, hd)
    y1_ref[0] = (_bmm(a_rk, v) - _bmm(a_rb, w1)).reshape(hb, tb, hd)
    bt = bh * g_last
    m_ref[0] = (jnp.where(eye, g_last, 0.0) - _bmm_tn(bt, p)).reshape(hb, tb // c, hd, hd)
    n_ref[0] = (_bmm_tn(kh * g_last, v) - _bmm_tn(bt, w1)).reshape(hb, tb // c, hd, hd)


def _rwkv_inter_body(q_ref, y1_ref, m_ref, n_ref, h0_ref, y_ref, hout_ref, h_scr):
    c = RWKV_CHUNK
    tb = pl.program_id(1)

    @pl.when(tb == 0)
    def _():
        h_scr[...] = h0_ref[0]

    state = h_scr[...]
    for ci in range(q_ref.shape[2] // c):
        sl = pl.ds(ci * c, c)
        y_ref[0, :, sl, :] = _bmm(q_ref[0, :, sl, :], state) + y1_ref[0, :, sl, :]
        state = _bmm(m_ref[0, :, ci], state) + n_ref[0, :, ci]
    h_scr[...] = state

    @pl.when(tb == pl.num_programs(1) - 1)
    def _():
        hout_ref[0] = state


def _rwkv_chunked(r, lw, k, v, kk, kka, s0, heads_per_step=8, tokens_intra=256, tokens_inter=256):
    n, h, t, hd = r.shape
    c = RWKV_CHUNK
    nc = t // c
    rows = pl.BlockSpec((1, heads_per_step, tokens_intra, hd), lambda i, g, j: (i, g, j, 0))
    mats = pl.BlockSpec((1, heads_per_step, tokens_intra // c, hd, hd), lambda i, g, j: (i, g, j, 0, 0))
    rshape = jax.ShapeDtypeStruct((n, h, t, hd), F32)
    mshape = jax.ShapeDtypeStruct((n, h, nc, hd, hd), F32)
    q, y1, m, nn = pl.pallas_call(
        _rwkv_intra_body,
        grid=(n, h // heads_per_step, t // tokens_intra),
        in_specs=[rows] * 6,
        out_specs=[rows, rows, mats, mats],
        out_shape=[rshape, rshape, mshape, mshape],
        compiler_params=_params("parallel", "parallel", "parallel"),
        name="rwkv_intra",
    )(r, lw, k, v, kk, kka)
    rows = pl.BlockSpec((1, h, tokens_inter, hd), lambda i, j: (i, 0, j, 0))
    mats = pl.BlockSpec((1, h, tokens_inter // c, hd, hd), lambda i, j: (i, 0, j, 0, 0))
    state = pl.BlockSpec((1, h, hd, hd), lambda i, j: (i, 0, 0, 0))
    y, h_out = pl.pallas_call(
        _rwkv_inter_body,
        grid=(n, t // tokens_inter),
        in_specs=[rows, rows, mats, mats, state],
        out_specs=[rows, state],
        out_shape=[rshape, jax.ShapeDtypeStruct((n, h, hd, hd), F32)],
        scratch_shapes=[pltpu.VMEM((h, hd, hd), F32)],
        compiler_params=_params("parallel", "arbitrary"),
        name="rwkv_inter",
    )(q, y1, m, nn, jnp.swapaxes(s0, -1, -2))
    return y, jnp.swapaxes(h_out, -1, -2)


def _rwkv_scan_body(r_ref, lw_ref, k_ref, kk_ref, kka_ref, vt_ref, s0_ref, yt_ref, sout_ref, s_scr, y_scr,
                    *, n_tokens):
    tb = pl.program_id(1)

    @pl.when(tb == 0)
    def _():
        s_scr[...] = s0_ref[0]

    y_scr[...] = jnp.zeros_like(y_scr)
    lane = lax.broadcasted_iota(jnp.int32, (1, 1, SCAN_BLOCK), 2)

    def step(u, carry):
        onehot = lane == u
        row = lambda ref: ref[0, :, pl.ds(u, 1), :]
        s = s_scr[...]
        v_col = jnp.sum(jnp.where(onehot, vt_ref[0], 0.0), axis=-1, keepdims=True)
        sa = jnp.sum(s * row(kk_ref), axis=-1, keepdims=True)
        s = s * jnp.exp(row(lw_ref)) - sa * row(kka_ref) + v_col * row(k_ref)
        y = jnp.sum(s * row(r_ref), axis=-1, keepdims=True)
        s_scr[...] = s
        y_scr[...] = jnp.where(onehot, y, y_scr[...])
        return carry

    lax.fori_loop(0, jnp.minimum(SCAN_BLOCK, n_tokens - tb * SCAN_BLOCK), step, 0)
    yt_ref[0] = y_scr[...]

    @pl.when(tb == pl.num_programs(1) - 1)
    def _():
        sout_ref[0] = s_scr[...]


def _rwkv_scan(r, lw, k, kk, kka, v, s0, n_tokens):
    n, h, tp, hd = r.shape
    vt = jnp.swapaxes(v, -1, -2)
    rows = pl.BlockSpec((1, h, SCAN_BLOCK, hd), lambda i, t: (i, 0, t, 0))
    cols = pl.BlockSpec((1, h, hd, SCAN_BLOCK), lambda i, t: (i, 0, 0, t))
    state = pl.BlockSpec((1, h, hd, hd), lambda i, t: (i, 0, 0, 0))
    yt, s_out = pl.pallas_call(
        functools.partial(_rwkv_scan_body, n_tokens=n_tokens),
        grid=(n, tp // SCAN_BLOCK),
        in_specs=[rows] * 5 + [cols, state],
        out_specs=[cols, state],
        out_shape=[jax.ShapeDtypeStruct((n, h, hd, tp), F32), jax.ShapeDtypeStruct((n, h, hd, hd), F32)],
        scratch_shapes=[pltpu.VMEM((h, hd, hd), F32), pltpu.VMEM((h, hd, SCAN_BLOCK), F32)],
        compiler_params=_params("parallel", "arbitrary"),
        name="rwkv_scan",
    )(r, lw, k, kk, kka, vt, s0)
    return jnp.swapaxes(yt, -1, -2), s_out


def _rwkv_post_body(y_ref, r_ref, k_ref, v_ref, g_ref, rk_ref, gng_ref, gnb_ref, o_ref):
    outs = []
    for h in range(C_HEADS):
        sl = slice(h * HEAD_DIM, (h + 1) * HEAD_DIM)
        y = y_ref[0, h]
        d = y - jnp.mean(y, axis=-1, keepdims=True)
        var = jnp.mean(d * d, axis=-1, keepdims=True)
        yn = d * lax.rsqrt(var + GN_EPS) * gng_ref[:, sl] + gnb_ref[:, sl]
        bonus = jnp.sum(r_ref[0, h] * k_ref[0, h] * rk_ref[:, sl], axis=-1, keepdims=True) * v_ref[0, h]
        outs.append(yn + bonus)
    o_ref[0] = (jnp.concatenate(outs, axis=-1) * g_ref[0]).astype(o_ref.dtype)


def _rwkv_post(y, r, k, v, g, r_k, gn_g, gn_b, tm):
    n, h, t, hd = y.shape
    c = C_WIDTH
    head = pl.BlockSpec((1, h, tm, hd), lambda i, j: (i, 0, j, 0))
    nat = pl.BlockSpec((1, tm, c), lambda i, j: (i, j, 0))
    par = pl.BlockSpec((1, c), lambda i, j: (0, 0))
    return pl.pallas_call(
        _rwkv_post_body,
        grid=(n, t // tm),
        in_specs=[head] * 4 + [nat] + [par] * 3,
        out_specs=nat,
        out_shape=jax.ShapeDtypeStruct((n, t, c), BF16),
        compiler_params=_params("parallel", "parallel"),
        name="rwkv_post",
    )(y, r, k, v, g, r_k.reshape(1, c), gn_g.reshape(1, c), gn_b.reshape(1, c))


def _mem_attn_body(x_ref, wq_ref, mk_ref, mv_ref, wo_ref, g_ref, beta_ref, o_ref):
    x = x_ref[0]
    q = jnp.dot(x.astype(BF16), wq_ref[...], preferred_element_type=F32)
    scale = MEM_HEAD_DIM ** -0.5
    heads = []
    for h in range(MEM_HEADS):
        sl = slice(h * MEM_HEAD_DIM, (h + 1) * MEM_HEAD_DIM)
        s = _bdot_nt(q[:, sl], mk_ref[0][:, sl]) * scale
        m = jnp.max(s, axis=-1, keepdims=True)
        p = jnp.exp(s - m)
        p = p / jnp.sum(p, axis=-1, keepdims=True)
        heads.append(_bdot(p, mv_ref[0][:, sl]))
    o = jnp.concatenate(heads, axis=-1)
    h_out = jnp.dot(o.astype(BF16), wo_ref[...], preferred_element_type=F32)
    o_ref[0] = _layer_norm(DEEPNORM_ALPHA * x + h_out, g_ref[...], beta_ref[...])


def _memory_attention(x, mem_k, mem_v, w_q, w_o, g, beta, tm):
    n, t, d = x.shape
    mt = mem_k.shape[1]
    full = lambda a, b: pl.BlockSpec((a, b), lambda i, j: (0, 0))
    xspec = pl.BlockSpec((1, tm, d), lambda i, j: (i, j, 0))
    mem = pl.BlockSpec((1, mt, MEM_WIDTH), lambda i, j: (i, 0, 0))
    return pl.pallas_call(
        _mem_attn_body,
        grid=(n, t // tm),
        in_specs=[xspec, full(d, MEM_WIDTH), mem, mem, full(MEM_WIDTH, d), full(1, d), full(1, d)],
        out_specs=xspec,
        out_shape=jax.ShapeDtypeStruct((n, t, d), F32),
        compiler_params=_params("parallel", "parallel"),
        name="mem_attn",
    )(x, w_q, mem_k, mem_v, w_o, g.reshape(1, d), beta.reshape(1, d))


def _router_body(x_ref, w_ref, b_ref, idx_ref, gate_ref):
    x = x_ref[...]
    w = w_ref[...]
    xh = x.astype(BF16)
    xl = (x - xh.astype(F32)).astype(BF16)
    wh = w.astype(BF16)
    wl = (w - wh.astype(F32)).astype(BF16)
    d = lambda a, b: jnp.dot(a, b, preferred_element_type=F32)
    logits = d(xh, wh) + d(xh, wl) + d(xl, wh) + b_ref[...]
    lane = lax.broadcasted_iota(jnp.int32, logits.shape, 1).astype(F32)
    neg = -jnp.inf
    work = jnp.where(lane < N_EXPERTS, logits, neg)
    idx_out = jnp.zeros(logits.shape, F32)
    val_out = jnp.full(logits.shape, neg, F32)
    for k in range(TOP_K):
        m = jnp.max(work, axis=-1, keepdims=True)
        first = jnp.min(jnp.where(work == m, lane, float(LANES)), axis=-1, keepdims=True)
        idx_out = jnp.where(lane == k, first, idx_out)
        val_out = jnp.where(lane == k, m, val_out)
        work = jnp.where(lane == first, neg, work)
    e = jnp.exp(val_out - jnp.max(val_out, axis=-1, keepdims=True))
    idx_ref[...] = idx_out.astype(jnp.int32)
    gate_ref[...] = e / jnp.sum(e, axis=-1, keepdims=True)


def _router(x, w, b, tm):
    rows, d = x.shape
    wp = jnp.pad(w, ((0, 0), (0, LANES - N_EXPERTS)))
    bp = jnp.pad(b, (0, LANES - N_EXPERTS)).reshape(1, LANES)
    out = pl.BlockSpec((tm, LANES), lambda i: (i, 0))
    idx, gates = pl.pallas_call(
        _router_body,
        grid=(rows // tm,),
        in_specs=[pl.BlockSpec((tm, d), lambda i: (i, 0)),
                  pl.BlockSpec((d, LANES), lambda i: (0, 0)),
                  pl.BlockSpec((1, LANES), lambda i: (0, 0))],
        out_specs=[out, out],
        out_shape=[jax.ShapeDtypeStruct((rows, LANES), jnp.int32), jax.ShapeDtypeStruct((rows, LANES), F32)],
        compiler_params=_params("parallel"),
        name="router",
    )(x, wp, bp)
    return idx[:, :TOP_K], gates[:, :TOP_K]


def _expert_up_body(be_ref, nu_ref, x_ref, wg_ref, wu_ref, bg_ref, bu_ref, h_ref):
    @pl.when(pl.program_id(1) < nu_ref[0])
    def _():
        x = x_ref[...].astype(BF16)
        gate = jnp.dot(x, wg_ref[0, 0].astype(BF16), preferred_element_type=F32) + bg_ref[0, 0]
        up = jnp.dot(x, wu_ref[0, 0].astype(BF16), preferred_element_type=F32) + bu_ref[0, 0]
        gate = jnp.minimum(gate, SWIGLU_LIMIT)
        up = jnp.clip(up, -SWIGLU_LIMIT, SWIGLU_LIMIT)
        h_ref[...] = ((up + 1.0) * gate * _sigmoid(SWIGLU_ALPHA * gate)).astype(h_ref.dtype)


def _expert_down_body(be_ref, nu_ref, h_ref, w_ref, b_ref, y_ref):
    @pl.when(pl.program_id(1) < nu_ref[0])
    def _():
        y_ref[...] = jnp.dot(h_ref[...], w_ref[0, 0].astype(BF16), preferred_element_type=F32) + b_ref[0, 0]


def _expert_ffn(rows, block_e, n_used, layer, w_gate_up, b_gate_up, w_down, b_down, tn):
    nr, d = rows.shape
    nb = nr // MOE_BLOCK
    nj = D_FF // tn
    b_gu = b_gate_up.reshape(DEPTH, N_EXPERTS, 1, 2 * D_FF)
    b_dn = b_down.reshape(DEPTH, N_EXPERTS, 1, d)
    hid = pl.pallas_call(
        _expert_up_body,
        grid_spec=pltpu.PrefetchScalarGridSpec(
            num_scalar_prefetch=2,
            grid=(nj, nb),
            in_specs=[pl.BlockSpec((MOE_BLOCK, d), lambda j, i, be, nu:(i, 0)),
                      pl.BlockSpec((1, 1, d, tn), lambda j, i, be, nu:(layer, be[i], 0, j)),
                      pl.BlockSpec((1, 1, d, tn), lambda j, i, be, nu:(layer, be[i], 0, j + nj)),
                      pl.BlockSpec((1, 1, 1, tn), lambda j, i, be, nu:(layer, be[i], 0, j)),
                      pl.BlockSpec((1, 1, 1, tn), lambda j, i, be, nu:(layer, be[i], 0, j + nj))],
            out_specs=pl.BlockSpec((MOE_BLOCK, tn), lambda j, i, be, nu:(i, j))),
        out_shape=jax.ShapeDtypeStruct((nr, D_FF), BF16),
        compiler_params=_params("parallel", "arbitrary"),
        name="expert_up",
    )(block_e, n_used, rows, w_gate_up, w_gate_up, b_gu, b_gu)
    nk = d // tn
    return pl.pallas_call(
        _expert_down_body,
        grid_spec=pltpu.PrefetchScalarGridSpec(
            num_scalar_prefetch=2,
            grid=(nk, nb),
            in_specs=[pl.BlockSpec((MOE_BLOCK, D_FF), lambda j, i, be, nu:(i, 0)),
                      pl.BlockSpec((1, 1, D_FF, tn), lambda j, i, be, nu:(layer, be[i], 0, j)),
                      pl.BlockSpec((1, 1, 1, tn), lambda j, i, be, nu:(layer, be[i], 0, j))],
            out_specs=pl.BlockSpec((MOE_BLOCK, tn), lambda j, i, be, nu:(i, j))),
        out_shape=jax.ShapeDtypeStruct((nr, d), F32),
        compiler_params=_params("parallel", "arbitrary"),
        name="expert_down",
    )(block_e, n_used, hid, w_down, b_dn)


def _moe_combine_body(x_ref, y_ref, gate_ref, g_ref, beta_ref, o_ref):
    gates = gate_ref[...]
    acc = gates[:, 0:1] * y_ref[0]
    for k in range(1, TOP_K):
        acc = acc + gates[:, k:k + 1] * y_ref[k]
    o_ref[...] = _layer_norm(DEEPNORM_ALPHA * x_ref[...] + acc, g_ref[...], beta_ref[...])


def _moe_combine(x, y_k, gates, g, beta, tm):
    rows, d = x.shape
    gp = jnp.pad(gates, ((0, 0), (0, LANES - TOP_K)))
    xs = pl.BlockSpec((tm, d), lambda i: (i, 0))
    par = pl.BlockSpec((1, d), lambda i: (0, 0))
    return pl.pallas_call(
        _moe_combine_body,
        grid=(rows // tm,),
        in_specs=[xs, pl.BlockSpec((TOP_K, tm, d), lambda i: (0, i, 0)),
                  pl.BlockSpec((tm, LANES), lambda i: (i, 0)), par, par],
        out_specs=xs,
        out_shape=jax.ShapeDtypeStruct((rows, d), F32),
        compiler_params=_params("parallel"),
        name="moe_combine",
    )(x, y_k, gp, g.reshape(1, d), beta.reshape(1, d))


def _moe_layer(x, layer, router_w, router_b, w_gate_up, b_gate_up, w_down, b_down, g, beta):
    rows, d = x.shape
    top_idx, gates = _router(x, router_w[layer], router_b[layer], 256)
    n_assign = rows * TOP_K
    flat_e = top_idx.reshape(-1)
    onehot = (flat_e[:, None] == jnp.arange(N_EXPERTS, dtype=jnp.int32)[None, :]).astype(jnp.int32)
    running = jnp.cumsum(onehot, axis=0)
    counts = running[-1]
    padded = (counts + MOE_BLOCK - 1) // MOE_BLOCK * MOE_BLOCK
    pad_end = jnp.cumsum(padded)
    pad_start = pad_end - padded
    dest = jnp.sum(onehot * (running - 1 + pad_start[None, :]), axis=1)
    n_blocks = -(-n_assign // MOE_BLOCK) + N_EXPERTS
    row_tok = jnp.zeros((n_blocks * MOE_BLOCK,), jnp.int32).at[dest].set(
        jnp.arange(n_assign, dtype=jnp.int32) // TOP_K)
    block_start = jnp.arange(n_blocks, dtype=jnp.int32) * MOE_BLOCK
    block_e = jnp.minimum(jnp.sum((pad_end[None, :] <= block_start[:, None]).astype(jnp.int32), axis=1),
                          N_EXPERTS - 1)
    n_used = (pad_end[-1:] // MOE_BLOCK).astype(jnp.int32)
    y_rows = _expert_ffn(x[row_tok], block_e, n_used, layer, w_gate_up, b_gate_up, w_down, b_down, 512)
    return _moe_combine(x, y_rows[dest.reshape(rows, TOP_K).T], gates, g, beta, 256)


def kernel(x_prompt, x_sample, cache_a_k, cache_a_v, cache_b_k, cache_b_v, state_c_wkv, state_c_shift,
           cache_mem_k, cache_mem_v, mem_prompt, w_in, a_sink, c_mu, c_w0, c_w_up, c_a0, c_a_up, c_g_up,
           c_k_k, c_k_a, c_r_k, c_gn_g, c_gn_b, w_out, ln_g, ln_b, w_mem_q, w_mem_k, w_mem_v, w_mem_o,
           router_w, router_b, w_gate_up, b_gate_up, w_down, b_down):
    n_p, t_p, d = x_prompt.shape
    n_s, t_s, _ = x_sample.shape
    la = cache_a_k.shape[2]
    lb = cache_b_k.shape[2]
    assert t_s == 1 and t_p % B_MAX_WINDOW == 0 and la == A_WINDOW and lb == B_MAX_WINDOW
    rows_p = n_p * t_p
    rows_all = _round_up(rows_p + n_s, 256)
    xp = x_prompt.reshape(rows_p, d)
    xs = x_sample.reshape(n_s, d)
    mem_bf = mem_prompt.reshape(n_p * MEM_TOKENS, d).astype(BF16)
    branches_b = tuple((lb - w, dil) for w, dil in B_BRANCHES)
    no_sink = jnp.full((1, B_HEADS, HEAD_DIM), -jnp.inf, F32)
    zero_state = jnp.zeros((n_p, C_HEADS, HEAD_DIM, HEAD_DIM), F32)
    states = []
    for l in range(DEPTH):
        cpar = (c_mu[l], c_w0[l], c_w_up[l], c_a0[l], c_a_up[l], c_g_up[l], c_k_k[l], c_k_a[l])
        w_in_bf = w_in[l].astype(BF16)
        w_ab, w_fc = w_in_bf[:, :AB_COLS], w_in_bf[:, AB_COLS:]
        w_out_bf = w_out[l].astype(BF16)
        w_q_bf = w_mem_q[l].astype(BF16)
        w_o_bf = w_mem_o[l].astype(BF16)
        r_k = c_r_k[l].reshape(-1)

        xp_bf = xp.astype(BF16)
        qkv = _matmul(xp_bf, w_ab, 512, AB_COLS // 3).reshape(n_p, t_p, AB_COLS)
        fc = _matmul(xp_bf, w_fc, 512, C_SHIFT_W // 2).reshape(n_p, t_p, C_SHIFT_W)
        oa = _swa_attention(qkv, a_sink[l])
        ob = _dilated_attention(qkv)
        r, lw, k, v, kk, kka, g = _rwkv_pre(fc, None, *cpar, 128)
        y, p_wkv = _rwkv_chunked(r, lw, k, v, kk, kka, zero_state)
        oc = _rwkv_post(y, r, k, v, g, r_k, c_gn_g[l], c_gn_b[l], 128)
        xp = _out_proj_ln(oa.reshape(rows_p, -1), ob.reshape(rows_p, -1), oc.reshape(rows_p, -1), w_out_bf, xp,
                          ln_g[l, 0], ln_b[l, 0], 256)
        mk = _matmul(mem_bf, w_mem_k[l].astype(BF16), 512, 512).reshape(n_p, MEM_TOKENS, MEM_WIDTH)
        mv = _matmul(mem_bf, w_mem_v[l].astype(BF16), 512, 512).reshape(n_p, MEM_TOKENS, MEM_WIDTH)
        xp = _memory_attention(xp.reshape(n_p, t_p, d), mk, mv, w_q_bf, w_o_bf,
                               ln_g[l, 1], ln_b[l, 1], 256).reshape(rows_p, d)
        ka = qkv[:, t_p - A_WINDOW:, A_HEADS * HEAD_DIM:A_HEADS * HEAD_DIM + A_KV_HEADS * HEAD_DIM]
        va = qkv[:, t_p - A_WINDOW:, A_HEADS * HEAD_DIM + A_KV_HEADS * HEAD_DIM:A_COLS]
        kb = qkv[:, t_p - B_MAX_WINDOW:, A_COLS + B_HEADS * HEAD_DIM:A_COLS + 2 * B_HEADS * HEAD_DIM]
        vb = qkv[:, t_p - B_MAX_WINDOW:, A_COLS + 2 * B_HEADS * HEAD_DIM:]
        p_items = (ka.reshape(n_p, A_WINDOW, A_KV_HEADS, HEAD_DIM), va.reshape(n_p, A_WINDOW, A_KV_HEADS, HEAD_DIM),
                   kb.reshape(n_p, B_MAX_WINDOW, B_HEADS, HEAD_DIM), vb.reshape(n_p, B_MAX_WINDOW, B_HEADS, HEAD_DIM),
                   p_wkv, fc[:, -1],
                   mk.reshape(n_p, MEM_TOKENS, MEM_HEADS, MEM_HEAD_DIM),
                   mv.reshape(n_p, MEM_TOKENS, MEM_HEADS, MEM_HEAD_DIM))

        proj = _matmul(xs.astype(BF16), w_in_bf, 32, 512)
        qkv_s, fc_s = proj[:, :AB_COLS], proj[:, AB_COLS:]
        c0 = A_HEADS * HEAD_DIM
        c1 = c0 + A_KV_HEADS * HEAD_DIM
        ka_s = qkv_s[:, c0:c1].reshape(n_s, A_KV_HEADS, HEAD_DIM)
        va_s = qkv_s[:, c1:A_COLS].reshape(n_s, A_KV_HEADS, HEAD_DIM)
        b0 = A_COLS + B_HEADS * HEAD_DIM
        b1 = b0 + B_HEADS * HEAD_DIM
        kb_s = qkv_s[:, b0:b1].reshape(n_s, B_HEADS, HEAD_DIM)
        vb_s = qkv_s[:, b1:].reshape(n_s, B_HEADS, HEAD_DIM)
        q_g = qkv_s[:, :c0].reshape(n_s, A_KV_HEADS, A_GROUP, HEAD_DIM).transpose(0, 2, 1, 3)
        sink_g = jnp.broadcast_to(a_sink[l].reshape(A_KV_HEADS, A_GROUP).T[:, :, None],
                                  (A_GROUP, A_KV_HEADS, HEAD_DIM))
        oa_s = _cache_attention(q_g, cache_a_k, cache_a_v, l, ka_s, va_s, sink_g, ((0, 1),))
        oa_s = oa_s.transpose(0, 2, 1, 3).reshape(n_s, -1)
        ob_s = _cache_attention(qkv_s[:, A_COLS:b0].reshape(n_s, 1, B_HEADS, HEAD_DIM), cache_b_k, cache_b_v, l,
                                kb_s, vb_s, no_sink, branches_b).reshape(n_s, -1)
        r, lw, k, v, kk, kka, g = _rwkv_pre(fc_s[None], state_c_shift[l][None], *cpar, n_s)
        seq = lambda z: jnp.pad(z[0].transpose(1, 0, 2)[:, :, None], ((0, 0), (0, 0), (0, SCAN_BLOCK - 1), (0, 0)))
        y_s, s_wkv = _rwkv_scan(seq(r), seq(lw), seq(k), seq(kk), seq(kka), seq(v), state_c_wkv[l], 1)
        oc_s = _rwkv_post(y_s[:, :, 0].transpose(1, 0, 2)[None], r, k, v, g, r_k, c_gn_g[l], c_gn_b[l], n_s)
        xs = _out_proj_ln(oa_s.astype(BF16), ob_s.astype(BF16), oc_s[0], w_out_bf, xs, ln_g[l, 0], ln_b[l, 0], n_s)
        xs_pad = jnp.pad(xs[:, None], ((0, 0), (0, SUBLANES - 1), (0, 0)))
        xs = _memory_attention(xs_pad, cache_mem_k[l].reshape(n_s, MEM_TOKENS, MEM_WIDTH),
                               cache_mem_v[l].reshape(n_s, MEM_TOKENS, MEM_WIDTH), w_q_bf, w_o_bf,
                               ln_g[l, 1], ln_b[l, 1], SUBLANES)[:, 0]
        s_items = (ka_s[:, None], va_s[:, None], kb_s[:, None], vb_s[:, None], s_wkv, fc_s)

        x_all = jnp.concatenate([xp, xs, jnp.zeros((rows_all - rows_p - n_s, d), F32)], axis=0)
        x_all = _moe_layer(x_all, l, router_w, router_b, w_gate_up, b_gate_up, w_down, b_down,
                           ln_g[l, 2], ln_b[l, 2])
        xp, xs = x_all[:rows_p], x_all[rows_p:rows_p + n_s]
        states.append(p_items + s_items)

    stacked = [jnp.stack(z, axis=0) for z in zip(*states)]
    return (xp.reshape(n_p, t_p, d), xs.reshape(n_s, t_s, d), *stacked)
```

```python
import functools
import math

import jax
import jax.numpy as jnp
from jax import lax
from jax.experimental import pallas as pl
from jax.experimental.pallas import tpu as pltpu

F32 = jnp.float32
BF16 = jnp.bfloat16

D_MODEL = 2048
DEPTH = 4
HEAD_DIM = 64
A_HEADS = 8
A_KV_HEADS = 2
A_GROUP = 4
A_WINDOW = 128
B_HEADS = 8
B_BRANCHES = ((128, 1), (512, 4), (2048, 16))
B_MAX_WINDOW = 2048
C_HEADS = 16
C_WIDTH = C_HEADS * HEAD_DIM
C_LORA_W = 64
C_LORA_A = 64
C_LORA_G = 128
C_SHIFT_W = 3 * C_WIDTH + C_LORA_W + C_LORA_A + C_LORA_G
A_COLS = (A_HEADS + 2 * A_KV_HEADS) * HEAD_DIM
B_COLS = 3 * B_HEADS * HEAD_DIM
AB_COLS = A_COLS + B_COLS
MEM_TOKENS = 256
MEM_HEADS = 4
MEM_HEAD_DIM = 128
MEM_WIDTH = MEM_HEADS * MEM_HEAD_DIM
N_EXPERTS = 32
TOP_K = 4
D_FF = D_MODEL
SWIGLU_ALPHA = 1.702
SWIGLU_LIMIT = 7.0
BAND = 128
LN_EPS = 1e-5
GN_EPS = 64e-5
DECAY_SCALE = math.exp(-0.5)
DEEPNORM_ALPHA = (2 * DEPTH) ** 0.25

LANES = 128
SUBLANES = 8
VMEM_LIMIT = 48 * 1024 * 1024
SCAN_BLOCK = 128
RWKV_CHUNK = 64
MOE_BLOCK = 512
PAIR = LANES // HEAD_DIM


def _params(*sem):
    return pltpu.CompilerParams(dimension_semantics=sem, vmem_limit_bytes=VMEM_LIMIT)


def _round_up(x, m):
    return (x + m - 1) // m * m


def _sigmoid(x):
    return 1.0 / (1.0 + jnp.exp(-x))


def _layer_norm(z, g, b):
    mu = jnp.mean(z, axis=-1, keepdims=True)
    d = z - mu
    var = jnp.mean(d * d, axis=-1, keepdims=True)
    return d * lax.rsqrt(var + LN_EPS) * g + b


def _split3(x):
    h = x.astype(BF16)
    r = x - h.astype(F32)
    m = r.astype(BF16)
    l = (r - m.astype(F32)).astype(BF16)
    return h, m, l


def _bdot(a, b):
    return jnp.dot(a.astype(BF16), b.astype(BF16), preferred_element_type=F32)


def _bdot_nt(a, b):
    return lax.dot_general(a.astype(BF16), b.astype(BF16), (((1,), (1,)), ((), ())), preferred_element_type=F32)


def _bmm(a, b):
    return lax.dot_general(a.astype(BF16), b.astype(BF16), (((2,), (1,)), ((0,), (0,))), preferred_element_type=F32)


def _bmm_nt(a, b):
    return lax.dot_general(a.astype(BF16), b.astype(BF16), (((2,), (2,)), ((0,), (0,))), preferred_element_type=F32)


def _bmm_tn(a, b):
    return lax.dot_general(a.astype(BF16), b.astype(BF16), (((1,), (1,)), ((0,), (0,))), preferred_element_type=F32)


def _mm_body(a_ref, b_ref, o_ref):
    o_ref[...] = jnp.dot(a_ref[...], b_ref[...], preferred_element_type=F32)


def _matmul(a, b, tm, tn):
    m, k = a.shape
    n = b.shape[1]
    mp = _round_up(m, tm)
    if mp != m:
        a = jnp.pad(a, ((0, mp - m), (0, 0)))
    out = pl.pallas_call(
        _mm_body,
        grid=(n // tn, mp // tm),
        in_specs=[pl.BlockSpec((tm, k), lambda j, i: (i, 0)),
                  pl.BlockSpec((k, tn), lambda j, i: (0, j))],
        out_specs=pl.BlockSpec((tm, tn), lambda j, i: (i, j)),
        out_shape=jax.ShapeDtypeStruct((mp, n), F32),
        compiler_params=_params("parallel", "parallel"),
        name="mm",
    )(a, b)
    return out[:m]


def _out_proj_body(oa_ref, ob_ref, oc_ref, wa_ref, wb_ref, wc_ref, res_ref, g_ref, beta_ref, o_ref):
    dot = lambda a, w: jnp.dot(a[...], w[...], preferred_element_type=F32)
    h = dot(oa_ref, wa_ref) + dot(ob_ref, wb_ref) + dot(oc_ref, wc_ref)
    o_ref[...] = _layer_norm(DEEPNORM_ALPHA * res_ref[...] + h, g_ref[...], beta_ref[...])


def _out_proj_ln(oa, ob, oc, w, res, g, beta, tm):
    m = oa.shape[0]
    d = w.shape[1]
    ka, kc = oa.shape[1], oc.shape[1]
    row = lambda width: pl.BlockSpec((tm, width), lambda i: (i, 0))
    par = pl.BlockSpec((1, d), lambda i: (0, 0))
    return pl.pallas_call(
        _out_proj_body,
        grid=(m // tm,),
        in_specs=[row(ka), row(ka), row(kc),
                  pl.BlockSpec((ka, d), lambda i: (0, 0)), pl.BlockSpec((ka, d), lambda i: (1, 0)),
                  pl.BlockSpec((kc, d), lambda i: (1, 0)), row(d), par, par],
        out_specs=row(d),
        out_shape=jax.ShapeDtypeStruct((m, d), F32),
        compiler_params=_params("parallel"),
        name="out_proj_ln",
    )(oa, ob, oc, w, w, w, res, g.reshape(1, d), beta.reshape(1, d))


def _band_bias():
    row = lax.broadcasted_iota(jnp.int32, (BAND, 2 * BAND), 0)
    col = lax.broadcasted_iota(jnp.int32, (BAND, 2 * BAND), 1)
    dist = jnp.where(col < BAND, BAND + row - col, row - (col - BAND))
    return jnp.where((dist >= 0) & (dist <= BAND), 0.0, -jnp.inf), col < BAND


def _softmax_pv(s, v_bf, sink=None):
    m = jnp.max(s, axis=-1, keepdims=True)
    if sink is not None:
        m = jnp.maximum(m, sink)
    p = jnp.exp(s - m)
    denom = jnp.sum(p, axis=-1, keepdims=True)
    if sink is not None:
        denom = denom + jnp.exp(sink - m)
    num = jnp.dot(p.astype(BF16), v_bf, preferred_element_type=F32)
    return num / denom, m + jnp.log(denom)


def _swa_body(sink_ref, q_ref, kp_ref, kc_ref, vp_ref, vc_ref, o_ref):
    b = pl.program_id(1)
    scale = HEAD_DIM ** -0.5
    bias, is_prev = _band_bias()
    bias = bias + jnp.where(is_prev, jnp.where(b > 0, 0.0, -jnp.inf), 0.0)
    q = q_ref[0]
    outs = []
    for kv in range(A_KV_HEADS):
        sl = slice(kv * HEAD_DIM, (kv + 1) * HEAD_DIM)
        k_cat = jnp.concatenate([kp_ref[0][:, sl], kc_ref[0][:, sl]], axis=0).astype(BF16)
        v_cat = jnp.concatenate([vp_ref[0][:, sl], vc_ref[0][:, sl]], axis=0).astype(BF16)
        for g in range(A_GROUP):
            h = kv * A_GROUP + g
            s = _bdot_nt(q[:, h * HEAD_DIM:(h + 1) * HEAD_DIM], k_cat) * scale + bias
            o, _ = _softmax_pv(s, v_cat, sink_ref[h])
            outs.append(o)
    o_ref[0] = jnp.concatenate(outs, axis=-1).astype(o_ref.dtype)


def _swa_attention(qkv, sink):
    n, t, _ = qkv.shape
    qw = A_HEADS * HEAD_DIM
    kblk = qw // LANES
    prev = lambda c: pl.BlockSpec((1, BAND, LANES), lambda i, b: (i, jnp.maximum(b - 1, 0), c))
    cur = lambda c: pl.BlockSpec((1, BAND, LANES), lambda i, b: (i, b, c))
    qspec = pl.BlockSpec((1, BAND, qw), lambda i, b: (i, b, 0))
    return pl.pallas_call(
        _swa_body,
        grid=(n, t // BAND),
        in_specs=[pl.BlockSpec(memory_space=pltpu.SMEM), qspec, prev(kblk), cur(kblk), prev(kblk + 1), cur(kblk + 1)],
        out_specs=qspec,
        out_shape=jax.ShapeDtypeStruct((n, t, qw), BF16),
        compiler_params=_params("parallel", "parallel"),
        name="swa_attn",
    )(sink.astype(F32), qkv, qkv, qkv, qkv, qkv)


def _dilated_body(q_ref, kp_ref, kc_ref, vp_ref, vc_ref, o_ref, kcat, vcat, ob_scr, lse_scr):
    span = B_MAX_WINDOW
    sb = pl.program_id(2)
    scale = HEAD_DIM ** -0.5
    kcat[0:span] = kp_ref[0]
    kcat[span:2 * span] = kc_ref[0]
    vcat[0:span] = vp_ref[0]
    vcat[span:2 * span] = vc_ref[0]
    bias0, is_prev = _band_bias()
    no_prev_span = jnp.where(sb > 0, 0.0, -jnp.inf)
    first_head = lax.broadcasted_iota(jnp.int32, (BAND, LANES), 1) < HEAD_DIM
    for bi, (window, dil) in enumerate(B_BRANCHES):
        nsub = span // dil // BAND

        def sub(i, carry, bi=bi, dil=dil, nsub=nsub):
            r = i // nsub
            j = i % nsub
            q_start = r + dil * BAND * j
            k_start = span + q_start - dil * BAND
            if dil == 1:
                q_rows, k_rows = pl.ds(q_start, BAND), pl.ds(k_start, 2 * BAND)
            else:
                q_rows = pl.ds(q_start, BAND, stride=dil)
                k_rows = pl.ds(k_start, 2 * BAND, stride=dil)
            q2 = q_ref.at[0][q_rows, :]
            k2 = kcat[k_rows, :]
            v2 = vcat[k_rows, :]
            bias = bias0 + jnp.where(is_prev, jnp.where(j == 0, no_prev_span, 0.0), 0.0)
            k_bf = k2.astype(BF16)
            v_bf = v2.astype(BF16)
            o_a, lse_a = _softmax_pv(_bdot_nt(jnp.where(first_head, q2, 0.0), k_bf) * scale + bias, v_bf)
            o_b, lse_b = _softmax_pv(_bdot_nt(jnp.where(first_head, 0.0, q2), k_bf) * scale + bias, v_bf)
            ob_scr.at[bi][q_rows, :] = jnp.where(first_head, o_a, o_b)
            lse_scr.at[bi][q_rows, :] = jnp.where(first_head, lse_a, lse_b)
            return carry

        lax.fori_loop(0, span // BAND, sub, 0, unroll=4)
    l0, l1, l2 = lse_scr[0], lse_scr[1], lse_scr[2]
    m = jnp.maximum(jnp.maximum(l0, l1), l2)
    e0, e1, e2 = jnp.exp(l0 - m), jnp.exp(l1 - m), jnp.exp(l2 - m)
    tot = e0 + e1 + e2
    o_ref[0] = ((e0 / tot) * ob_scr[0] + (e1 / tot) * ob_scr[1] + (e2 / tot) * ob_scr[2]).astype(o_ref.dtype)


def _dilated_attention(qkv):
    n, t, _ = qkv.shape
    span = B_MAX_WINDOW
    width = B_HEADS * HEAD_DIM
    q0 = A_COLS // LANES
    nhp = width // LANES
    blk = lambda c0, prev: pl.BlockSpec(
        (1, span, LANES),
        (lambda i, p, s: (i, jnp.maximum(s - 1, 0), c0 + p)) if prev else (lambda i, p, s: (i, s, c0 + p)))
    return pl.pallas_call(
        _dilated_body,
        grid=(n, nhp, t // span),
        in_specs=[blk(q0, False), blk(q0 + nhp, True), blk(q0 + nhp, False),
                  blk(q0 + 2 * nhp, True), blk(q0 + 2 * nhp, False)],
        out_specs=pl.BlockSpec((1, span, LANES), lambda i, p, s: (i, s, p)),
        out_shape=jax.ShapeDtypeStruct((n, t, width), BF16),
        scratch_shapes=[pltpu.VMEM((2 * span, LANES), F32), pltpu.VMEM((2 * span, LANES), F32),
                        pltpu.VMEM((len(B_BRANCHES), span, LANES), F32),
                        pltpu.VMEM((len(B_BRANCHES), span, LANES), F32)],
        compiler_params=_params("parallel", "parallel", "parallel"),
        name="dilated_attn",
    )(qkv, qkv, qkv, qkv, qkv)


def _cache_attn_body(q_ref, kc_ref, vc_ref, kn_ref, vn_ref, sink_ref, o_ref, *, branches, group):
    scale = HEAD_DIM ** -0.5
    k_new = kn_ref[0]
    v_new = vn_ref[0]
    for g in range(group):
        q = q_ref[0, g]
        sink = sink_ref[g]
        s_n = jnp.sum(k_new * q, axis=-1, keepdims=True) * scale
        outs, lses = [], []
        for start, stride in branches:
            rows = pl.ds(start, BAND) if stride == 1 else pl.ds(start, BAND, stride=stride)
            k_c = kc_ref.at[0, 0][rows]
            v_c = vc_ref.at[0, 0][rows]
            s_c = jnp.sum(k_c * q[None], axis=-1, keepdims=True) * scale
            m = jnp.maximum(jnp.maximum(jnp.max(s_c, axis=0), s_n), sink)
            p_c = jnp.exp(s_c - m[None])
            p_n = jnp.exp(s_n - m)
            denom = jnp.sum(p_c, axis=0) + p_n + jnp.exp(sink - m)
            num = jnp.sum(p_c * v_c, axis=0) + p_n * v_new
            outs.append(num / denom)
            lses.append(m + jnp.log(denom))
        if len(branches) == 1:
            o = outs[0]
        else:
            m = functools.reduce(jnp.maximum, lses)
            es = [jnp.exp(l - m) for l in lses]
            tot = functools.reduce(lambda a, b: a + b, es)
            o = functools.reduce(lambda a, b: a + b, [(e / tot) * o_b for e, o_b in zip(es, outs)])
        o_ref[0, g] = o


def _cache_attention(q, k_cache, v_cache, layer, k_new, v_new, sink, branches):
    n, group, hk, hd = q.shape
    lc = k_cache.shape[2]
    cache = pl.BlockSpec((1, 1, lc, hk, hd), lambda i: (layer, i, 0, 0, 0), pipeline_mode=pl.Buffered(1))
    new = pl.BlockSpec((1, hk, hd), lambda i: (i, 0, 0))
    qspec = pl.BlockSpec((1, group, hk, hd), lambda i: (i, 0, 0, 0))
    return pl.pallas_call(
        functools.partial(_cache_attn_body, branches=branches, group=group),
        grid=(n,),
        in_specs=[qspec, cache, cache, new, new, pl.BlockSpec((group, hk, hd), lambda i: (0, 0, 0))],
        out_specs=qspec,
        out_shape=jax.ShapeDtypeStruct((n, group, hk, hd), F32),
        compiler_params=_params("parallel"),
        name="cache_attn",
    )(q, k_cache, v_cache, k_new, v_new, sink)


def _rwkv_pre_body(fc_ref, prev_ref, mu_ref, w0_ref, wup_ref, a0_ref, aup_ref, gup_ref, kk_w_ref, ka_w_ref,
                   r_ref, lw_ref, k_ref, v_ref, kk_ref, kka_ref, g_ref, *, shift_in_kernel):
    fc = fc_ref[0]
    if shift_in_kernel:
        last = jnp.where(pl.program_id(1) > 0, prev_ref[0][SUBLANES - 1:SUBLANES, :], 0.0)
        rowid = lax.broadcasted_iota(jnp.int32, fc.shape, 0)
        shifted = jnp.where(rowid == 0, last, pltpu.roll(fc, 1, 0))
    else:
        shifted = prev_ref[0]
    f = fc + (shifted - fc) * mu_ref[...]
    c = C_WIDTH
    r = f[:, 0:c]
    k = f[:, c:2 * c]
    v = f[:, 2 * c:3 * c]
    wd = f[:, 3 * c:3 * c + C_LORA_W]
    ad = f[:, 3 * c + C_LORA_W:3 * c + C_LORA_W + C_LORA_A]
    gd = f[:, 3 * c + C_LORA_W + C_LORA_A:]
    dot = lambda x, w_ref_: jnp.dot(x.astype(BF16), w_ref_[...], preferred_element_type=F32)
    log_decay = -DECAY_SCALE * _sigmoid(w0_ref[...] + dot(jnp.tanh(wd), wup_ref))
    a = _sigmoid(a0_ref[...] + dot(ad, aup_ref))
    g_ref[0] = dot(_sigmoid(gd), gup_ref)
    kk = k * kk_w_ref[...]
    k2 = k * (1.0 + (a - 1.0) * ka_w_ref[...])
    for h in range(C_HEADS):
        sl = slice(h * HEAD_DIM, (h + 1) * HEAD_DIM)
        kk_h = kk[:, sl]
        kk_h = kk_h / jnp.maximum(jnp.sqrt(jnp.sum(kk_h * kk_h, axis=-1, keepdims=True)), 1e-12)
        r_ref[0, h] = r[:, sl]
        lw_ref[0, h] = log_decay[:, sl]
        k_ref[0, h] = k2[:, sl]
        v_ref[0, h] = v[:, sl]
        kk_ref[0, h] = kk_h
        kka_ref[0, h] = kk_h * a[:, sl]


def _rwkv_pre(fc, shifted, mu, w0, w_up, a0, a_up, g_up, k_k, k_a, tm):
    n, t, _ = fc.shape
    c = C_WIDTH
    in_kernel = shifted is None
    fspec = pl.BlockSpec((1, tm, C_SHIFT_W), lambda i, j: (i, j, 0))
    if in_kernel:
        per = tm // SUBLANES
        pspec = pl.BlockSpec((1, SUBLANES, C_SHIFT_W), lambda i, j: (i, jnp.maximum(j * per - 1, 0), 0))
        shifted = fc
    else:
        pspec = fspec
    full = lambda a, b: pl.BlockSpec((a, b), lambda i, j: (0, 0))
    head = pl.BlockSpec((1, C_HEADS, tm, HEAD_DIM), lambda i, j: (i, 0, j, 0))
    hshape = jax.ShapeDtypeStruct((n, C_HEADS, t, HEAD_DIM), F32)
    return pl.pallas_call(
        functools.partial(_rwkv_pre_body, shift_in_kernel=in_kernel),
        grid=(n, t // tm),
        in_specs=[fspec, pspec, full(1, C_SHIFT_W), full(1, c), full(C_LORA_W, c),
                  full(1, c), full(C_LORA_A, c), full(C_LORA_G, c), full(1, c), full(1, c)],
        out_specs=[head] * 6 + [pl.BlockSpec((1, tm, c), lambda i, j: (i, j, 0))],
        out_shape=[hshape] * 6 + [jax.ShapeDtypeStruct((n, t, c), F32)],
        compiler_params=_params("parallel", "parallel"),
        name="rwkv_pre",
    )(fc, shifted, mu.reshape(1, -1), w0.reshape(1, c), w_up.astype(BF16), a0.reshape(1, c),
      a_up.astype(BF16), g_up.astype(BF16), k_k.reshape(1, c), k_a.reshape(1, c))


def _rwkv_intra_body(r_ref, lw_ref, k_ref, v_ref, kk_ref, kka_ref, q_ref, y1_ref, m_ref, n_ref):
    c = RWKV_CHUNK
    _, hb, tb, hd = r_ref.shape
    nb = hb * (tb // c)
    load = lambda ref: ref[0].reshape(nb, c, hd)
    row = lax.broadcasted_iota(jnp.int32, (nb, c, c), 1)
    col = lax.broadcasted_iota(jnp.int32, (nb, c, c), 2)
    incl = col <= row
    strict = col < row
    eye = col == row
    incl_bf = incl.astype(BF16)
    same = lambda bits: (row >> bits) == (col >> bits)
    lw = load(lw_ref)
    cum = functools.reduce(lambda a, b: a + b, [_bmm(incl_bf, part) for part in _split3(lw)])
    e_neg = jnp.exp(-cum)
    g_last = jnp.exp(cum[:, c - 1:c, :])
    kc = load(kk_ref) * jnp.exp(cum - lw)
    rh = load(r_ref) * jnp.exp(cum)
    kh = load(k_ref) * e_neg
    bh = load(kka_ref) * e_neg
    v = load(v_ref)
    a_kb = jnp.where(strict, _bmm_nt(kc, bh), 0.0)
    a_kk = jnp.where(strict, _bmm_nt(kc, kh), 0.0)
    a_rb = jnp.where(incl, _bmm_nt(rh, bh), 0.0)
    a_rk = jnp.where(incl, _bmm_nt(rh, kh), 0.0)
    l1 = jnp.where(same(3), a_kb, 0.0)
    l2 = _bmm(l1, l1)
    l4 = _bmm(l2, l2)
    t = jnp.where(eye, 1.0, 0.0) - l1
    t = t + _bmm(t, l2)
    t = t + _bmm(t, l4)
    for bits in (4, 5, 6):
        off = jnp.where(same(bits) & jnp.logical_not(same(bits - 1)), a_kb, 0.0)
        t = t - _bmm(_bmm(t, off), t)
    p = _bmm(t, kc)
    w1 = _bmm(t, _bmm(a_kk, v))
    q_ref[0] = (rh - _bmm(a_rb, p)).reshape(hb, tb, hd)
    y1_ref[0] = (_bmm(a_rk, v) - _bmm(a_rb, w1)).reshape(hb, tb, hd)
    bt = bh * g_last
    m_ref[0] = (jnp.where(eye, g_last, 0.0) - _bmm_tn(bt, p)).reshape(hb, tb // c, hd, hd)
    n_ref[0] = (_bmm_tn(kh * g_last, v) - _bmm_tn(bt, w1)).reshape(hb, tb // c, hd, hd)


def _rwkv_inter_body(q_ref, y1_ref, m_ref, n_ref, h0_ref, y_ref, hout_ref, h_scr):
    c = RWKV_CHUNK
    tb = pl.program_id(1)

    @pl.when(tb == 0)
    def _():
        h_scr[...] = h0_ref[0]

    state = h_scr[...]
    for ci in range(q_ref.shape[2] // c):
        sl = pl.ds(ci * c, c)
        y_ref[0, :, sl, :] = _bmm(q_ref[0, :, sl, :], state) + y1_ref[0, :, sl, :]
        state = _bmm(m_ref[0, :, ci], state) + n_ref[0, :, ci]
    h_scr[...] = state

    @pl.when(tb == pl.num_programs(1) - 1)
    def _():
        hout_ref[0] = state


def _rwkv_chunked(r, lw, k, v, kk, kka, s0, heads_per_step=8, tokens_intra=256, tokens_inter=256):
    n, h, t, hd = r.shape
    c = RWKV_CHUNK
    nc = t // c
    rows = pl.BlockSpec((1, heads_per_step, tokens_intra, hd), lambda i, g, j: (i, g, j, 0))
    mats = pl.BlockSpec((1, heads_per_step, tokens_intra // c, hd, hd), lambda i, g, j: (i, g, j, 0, 0))
    rshape = jax.ShapeDtypeStruct((n, h, t, hd), F32)
    mshape = jax.ShapeDtypeStruct((n, h, nc, hd, hd), F32)
    q, y1, m, nn = pl.pallas_call(
        _rwkv_intra_body,
        grid=(n, h // heads_per_step, t // tokens_intra),
        in_specs=[rows] * 6,
        out_specs=[rows, rows, mats, mats],
        out_shape=[rshape, rshape, mshape, mshape],
        compiler_params=_params("parallel", "parallel", "parallel"),
        name="rwkv_intra",
    )(r, lw, k, v, kk, kka)
    rows = pl.BlockSpec((1, h, tokens_inter, hd), lambda i, j: (i, 0, j, 0))
    mats = pl.BlockSpec((1, h, tokens_inter // c, hd, hd), lambda i, j: (i, 0, j, 0, 0))
    state = pl.BlockSpec((1, h, hd, hd), lambda i, j: (i, 0, 0, 0))
    y, h_out = pl.pallas_call(
        _rwkv_inter_body,
        grid=(n, t // tokens_inter),
        in_specs=[rows, rows, mats, mats, state],
        out_specs=[rows, state],
        out_shape=[rshape, jax.ShapeDtypeStruct((n, h, hd, hd), F32)],
        scratch_shapes=[pltpu.VMEM((h, hd, hd), F32)],
        compiler_params=_params("parallel", "arbitrary"),
        name="rwkv_inter",
    )(q, y1, m, nn, jnp.swapaxes(s0, -1, -2))
    return y, jnp.swapaxes(h_out, -1, -2)


def _rwkv_scan_body(r_ref, lw_ref, k_ref, kk_ref, kka_ref, vt_ref, s0_ref, yt_ref, sout_ref, s_scr, y_scr,
                    *, n_tokens):
    tb = pl.program_id(1)

    @pl.when(tb == 0)
    def _():
        s_scr[...] = s0_ref[0]

    y_scr[...] = jnp.zeros_like(y_scr)
    lane = lax.broadcasted_iota(jnp.int32, (1, 1, SCAN_BLOCK), 2)

    def step(u, carry):
        onehot = lane == u
        row = lambda ref: ref[0, :, pl.ds(u, 1), :]
        s = s_scr[...]
        v_col = jnp.sum(jnp.where(onehot, vt_ref[0], 0.0), axis=-1, keepdims=True)
        sa = jnp.sum(s * row(kk_ref), axis=-1, keepdims=True)
        s = s * jnp.exp(row(lw_ref)) - sa * row(kka_ref) + v_col * row(k_ref)
        y = jnp.sum(s * row(r_ref), axis=-1, keepdims=True)
        s_scr[...] = s
        y_scr[...] = jnp.where(onehot, y, y_scr[...])
        return carry

    lax.fori_loop(0, jnp.minimum(SCAN_BLOCK, n_tokens - tb * SCAN_BLOCK), step, 0)
    yt_ref[0] = y_scr[...]

    @pl.when(tb == pl.num_programs(1) - 1)
    def _():
        sout_ref[0] = s_scr[...]


def _rwkv_scan(r, lw, k, kk, kka, v, s0, n_tokens):
    n, h, tp, hd = r.shape
    vt = jnp.swapaxes(v, -1, -2)
    rows = pl.BlockSpec((1, h, SCAN_BLOCK, hd), lambda i, t: (i, 0, t, 0))
    cols = pl.BlockSpec((1, h, hd, SCAN_BLOCK), lambda i, t: (i, 0, 0, t))
    state = pl.BlockSpec((1, h, hd, hd), lambda i, t: (i, 0, 0, 0))
    yt, s_out = pl.pallas_call(
        functools.partial(_rwkv_scan_body, n_tokens=n_tokens),
        grid=(n, tp // SCAN_BLOCK),
        in_specs=[rows] * 5 + [cols, state],
        out_specs=[cols, state],
        out_shape=[jax.ShapeDtypeStruct((n, h, hd, tp), F32), jax.ShapeDtypeStruct((n, h, hd, hd), F32)],
        scratch_shapes=[pltpu.VMEM((h, hd, hd), F32), pltpu.VMEM((h, hd, SCAN_BLOCK), F32)],
        compiler_params=_params("parallel", "arbitrary"),
        name="rwkv_scan",
    )(r, lw, k, kk, kka, vt, s0)
    return jnp.swapaxes(yt, -1, -2), s_out


def _rwkv_post_body(y_ref, r_ref, k_ref, v_ref, g_ref, rk_ref, gng_ref, gnb_ref, o_ref):
    outs = []
    for h in range(C_HEADS):
        sl = slice(h * HEAD_DIM, (h + 1) * HEAD_DIM)
        y = y_ref[0, h]
        d = y - jnp.mean(y, axis=-1, keepdims=True)
        var = jnp.mean(d * d, axis=-1, keepdims=True)
        yn = d * lax.rsqrt(var + GN_EPS) * gng_ref[:, sl] + gnb_ref[:, sl]
        bonus = jnp.sum(r_ref[0, h] * k_ref[0, h] * rk_ref[:, sl], axis=-1, keepdims=True) * v_ref[0, h]
        outs.append(yn + bonus)
    o_ref[0] = (jnp.concatenate(outs, axis=-1) * g_ref[0]).astype(o_ref.dtype)


def _rwkv_post(y, r, k, v, g, r_k, gn_g, gn_b, tm):
    n, h, t, hd = y.shape
    c = C_WIDTH
    head = pl.BlockSpec((1, h, tm, hd), lambda i, j: (i, 0, j, 0))
    nat = pl.BlockSpec((1, tm, c), lambda i, j: (i, j, 0))
    par = pl.BlockSpec((1, c), lambda i, j: (0, 0))
    return pl.pallas_call(
        _rwkv_post_body,
        grid=(n, t // tm),
        in_specs=[head] * 4 + [nat] + [par] * 3,
        out_specs=nat,
        out_shape=jax.ShapeDtypeStruct((n, t, c), BF16),
        compiler_params=_params("parallel", "parallel"),
        name="rwkv_post",
    )(y, r, k, v, g, r_k.reshape(1, c), gn_g.reshape(1, c), gn_b.reshape(1, c))


def _mem_attn_body(x_ref, wq_ref, mk_ref, mv_ref, wo_ref, g_ref, beta_ref, o_ref):
    x = x_ref[0]
    q = jnp.dot(x.astype(BF16), wq_ref[...], preferred_element_type=F32)
    scale = MEM_HEAD_DIM ** -0.5
    heads = []
    for h in range(MEM_HEADS):
        sl = slice(h * MEM_HEAD_DIM, (h + 1) * MEM_HEAD_DIM)
        s = _bdot_nt(q[:, sl], mk_ref[0][:, sl]) * scale
        m = jnp.max(s, axis=-1, keepdims=True)
        p = jnp.exp(s - m)
        p = p / jnp.sum(p, axis=-1, keepdims=True)
        heads.append(_bdot(p, mv_ref[0][:, sl]))
    o = jnp.concatenate(heads, axis=-1)
    h_out = jnp.dot(o.astype(BF16), wo_ref[...], preferred_element_type=F32)
    o_ref[0] = _layer_norm(DEEPNORM_ALPHA * x + h_out, g_ref[...], beta_ref[...])


def _memory_attention(x, mem_k, mem_v, w_q, w_o, g, beta, tm):
    n, t, d = x.shape
    mt = mem_k.shape[1]
    full = lambda a, b: pl.BlockSpec((a, b), lambda i, j: (0, 0))
    xspec = pl.BlockSpec((1, tm, d), lambda i, j: (i, j, 0))
    mem = pl.BlockSpec((1, mt, MEM_WIDTH), lambda i, j: (i, 0, 0))
    return pl.pallas_call(
        _mem_attn_body,
        grid=(n, t // tm),
        in_specs=[xspec, full(d, MEM_WIDTH), mem, mem, full(MEM_WIDTH, d), full(1, d), full(1, d)],
        out_specs=xspec,
        out_shape=jax.ShapeDtypeStruct((n, t, d), F32),
        compiler_params=_params("parallel", "parallel"),
        name="mem_attn",
    )(x, w_q, mem_k, mem_v, w_o, g.reshape(1, d), beta.reshape(1, d))


def _router_body(x_ref, w_ref, b_ref, idx_ref, gate_ref):
    x = x_ref[...]
    w = w_ref[...]
    xh = x.astype(BF16)
    xl = (x - xh.astype(F32)).astype(BF16)
    wh = w.astype(BF16)
    wl = (w - wh.astype(F32)).astype(BF16)
    d = lambda a, b: jnp.dot(a, b, preferred_element_type=F32)
    logits = d(xh, wh) + d(xh, wl) + d(xl, wh) + b_ref[...]
    lane = lax.broadcasted_iota(jnp.int32, logits.shape, 1).astype(F32)
    neg = -jnp.inf
    work = jnp.where(lane < N_EXPERTS, logits, neg)
    idx_out = jnp.zeros(logits.shape, F32)
    val_out = jnp.full(logits.shape, neg, F32)
    for k in range(TOP_K):
        m = jnp.max(work, axis=-1, keepdims=True)
        first = jnp.min(jnp.where(work == m, lane, float(LANES)), axis=-1, keepdims=True)
        idx_out = jnp.where(lane == k, first, idx_out)
        val_out = jnp.where(lane == k, m, val_out)
        work = jnp.where(lane == first, neg, work)
    e = jnp.exp(val_out - jnp.max(val_out, axis=-1, keepdims=True))
    idx_ref[...] = idx_out.astype(jnp.int32)
    gate_ref[...] = e / jnp.sum(e, axis=-1, keepdims=True)


def _router(x, w, b, tm):
    rows, d = x.shape
    wp = jnp.pad(w, ((0, 0), (0, LANES - N_EXPERTS)))
    bp = jnp.pad(b, (0, LANES - N_EXPERTS)).reshape(1, LANES)
    out = pl.BlockSpec((tm, LANES), lambda i: (i, 0))
    idx, gates = pl.pallas_call(
        _router_body,
        grid=(rows // tm,),
        in_specs=[pl.BlockSpec((tm, d), lambda i: (i, 0)),
                  pl.BlockSpec((d, LANES), lambda i: (0, 0)),
                  pl.BlockSpec((1, LANES), lambda i: (0, 0))],
        out_specs=[out, out],
        out_shape=[jax.ShapeDtypeStruct((rows, LANES), jnp.int32), jax.ShapeDtypeStruct((rows, LANES), F32)],
        compiler_params=_params("parallel"),
        name="router",
    )(x, wp, bp)
    return idx[:, :TOP_K], gates[:, :TOP_K]


def _expert_up_body(be_ref, nu_ref, x_ref, wg_ref, wu_ref, bg_ref, bu_ref, h_ref):
    @pl.when(pl.program_id(1) < nu_ref[0])
    def _():
        x = x_ref[...].astype(BF16)
        gate = jnp.dot(x, wg_ref[0, 0].astype(BF16), preferred_element_type=F32) + bg_ref[0, 0]
        up = jnp.dot(x, wu_ref[0, 0].astype(BF16), preferred_element_type=F32) + bu_ref[0, 0]
        gate = jnp.minimum(gate, SWIGLU_LIMIT)
        up = jnp.clip(up, -SWIGLU_LIMIT, SWIGLU_LIMIT)
        h_ref[...] = ((up + 1.0) * gate * _sigmoid(SWIGLU_ALPHA * gate)).astype(h_ref.dtype)


def _expert_down_body(be_ref, nu_ref, h_ref, w_ref, b_ref, y_ref):
    @pl.when(pl.program_id(1) < nu_ref[0])
    def _():
        y_ref[...] = jnp.dot(h_ref[...], w_ref[0, 0].astype(BF16), preferred_element_type=F32) + b_ref[0, 0]


def _expert_ffn(rows, block_e, n_used, layer, w_gate_up, b_gate_up, w_down, b_down, tn):
    nr, d = rows.shape
    nb = nr // MOE_BLOCK
    nj = D_FF // tn
    b_gu = b_gate_up.reshape(DEPTH, N_EXPERTS, 1, 2 * D_FF)
    b_dn = b_down.reshape(DEPTH, N_EXPERTS, 1, d)
    hid = pl.pallas_call(
        _expert_up_body,
        grid_spec=pltpu.PrefetchScalarGridSpec(
            num_scalar_prefetch=2,
            grid=(nj, nb),
            in_specs=[pl.BlockSpec((MOE_BLOCK, d), lambda j, i, be, nu:(i, 0)),
                      pl.BlockSpec((1, 1, d, tn), lambda j, i, be, nu:(layer, be[i], 0, j)),
                      pl.BlockSpec((1, 1, d, tn), lambda j, i, be, nu:(layer, be[i], 0, j + nj)),
                      pl.BlockSpec((1, 1, 1, tn), lambda j, i, be, nu:(layer, be[i], 0, j)),
                      pl.BlockSpec((1, 1, 1, tn), lambda j, i, be, nu:(layer, be[i], 0, j + nj))],
            out_specs=pl.BlockSpec((MOE_BLOCK, tn), lambda j, i, be, nu:(i, j))),
        out_shape=jax.ShapeDtypeStruct((nr, D_FF), BF16),
        compiler_params=_params("parallel", "arbitrary"),
        name="expert_up",
    )(block_e, n_used, rows, w_gate_up, w_gate_up, b_gu, b_gu)
    tn = 2 * tn
    nk = d // tn
    return pl.pallas_call(
        _expert_down_body,
        grid_spec=pltpu.PrefetchScalarGridSpec(
            num_scalar_prefetch=2,
            grid=(nk, nb),
            in_specs=[pl.BlockSpec((MOE_BLOCK, D_FF), lambda j, i, be, nu:(i, 0)),
                      pl.BlockSpec((1, 1, D_FF, tn), lambda j, i, be, nu:(layer, be[i], 0, j)),
                      pl.BlockSpec((1, 1, 1, tn), lambda j, i, be, nu:(layer, be[i], 0, j))],
            out_specs=pl.BlockSpec((MOE_BLOCK, tn), lambda j, i, be, nu:(i, j))),
        out_shape=jax.ShapeDtypeStruct((nr, d), F32),
        compiler_params=_params("parallel", "arbitrary"),
        name="expert_down",
    )(block_e, n_used, hid, w_down, b_dn)


def _moe_combine_body(x_ref, y_ref, gate_ref, g_ref, beta_ref, o_ref):
    gates = gate_ref[...]
    acc = gates[:, 0:1] * y_ref[0]
    for k in range(1, TOP_K):
        acc = acc + gates[:, k:k + 1] * y_ref[k]
    o_ref[...] = _layer_norm(DEEPNORM_ALPHA * x_ref[...] + acc, g_ref[...], beta_ref[...])


def _moe_combine(x, y_k, gates, g, beta, tm):
    rows, d = x.shape
    gp = jnp.pad(gates, ((0, 0), (0, LANES - TOP_K)))
    xs = pl.BlockSpec((tm, d), lambda i: (i, 0))
    par = pl.BlockSpec((1, d), lambda i: (0, 0))
    return pl.pallas_call(
        _moe_combine_body,
        grid=(rows // tm,),
        in_specs=[xs, pl.BlockSpec((TOP_K, tm, d), lambda i: (0, i, 0)),
                  pl.BlockSpec((tm, LANES), lambda i: (i, 0)), par, par],
        out_specs=xs,
        out_shape=jax.ShapeDtypeStruct((rows, d), F32),
        compiler_params=_params("parallel"),
        name="moe_combine",
    )(x, y_k, gp, g.reshape(1, d), beta.reshape(1, d))


def _moe_layer(x, layer, router_w, router_b, w_gate_up, b_gate_up, w_down, b_down, g, beta):
    rows, d = x.shape
    top_idx, gates = _router(x, router_w[layer], router_b[layer], 256)
    n_assign = rows * TOP_K
    flat_e = top_idx.reshape(-1)
    onehot = (flat_e[:, None] == jnp.arange(N_EXPERTS, dtype=jnp.int32)[None, :]).astype(jnp.int32)
    running = jnp.cumsum(onehot, axis=0)
    counts = running[-1]
    padded = (counts + MOE_BLOCK - 1) // MOE_BLOCK * MOE_BLOCK
    pad_end = jnp.cumsum(padded)
    pad_start = pad_end - padded
    dest = jnp.sum(onehot * (running - 1 + pad_start[None, :]), axis=1)
    n_blocks = -(-n_assign // MOE_BLOCK) + N_EXPERTS
    row_tok = jnp.zeros((n_blocks * MOE_BLOCK,), jnp.int32).at[dest].set(
        jnp.arange(n_assign, dtype=jnp.int32) // TOP_K)
    block_start = jnp.arange(n_blocks, dtype=jnp.int32) * MOE_BLOCK
    block_e = jnp.minimum(jnp.sum((pad_end[None, :] <= block_start[:, None]).astype(jnp.int32), axis=1),
                          N_EXPERTS - 1)
    n_used = (pad_end[-1:] // MOE_BLOCK).astype(jnp.int32)
    y_rows = _expert_ffn(x[row_tok], block_e, n_used, layer, w_gate_up, b_gate_up, w_down, b_down, 512)
    return _moe_combine(x, y_rows[dest.reshape(rows, TOP_K).T], gates, g, beta, 256)


def kernel(x_prompt, x_sample, cache_a_k, cache_a_v, cache_b_k, cache_b_v, state_c_wkv, state_c_shift,
           cache_mem_k, cache_mem_v, mem_prompt, w_in, a_sink, c_mu, c_w0, c_w_up, c_a0, c_a_up, c_g_up,
           c_k_k, c_k_a, c_r_k, c_gn_g, c_gn_b, w_out, ln_g, ln_b, w_mem_q, w_mem_k, w_mem_v, w_mem_o,
           router_w, router_b, w_gate_up, b_gate_up, w_down, b_down):
    n_p, t_p, d = x_prompt.shape
    n_s, t_s, _ = x_sample.shape
    la = cache_a_k.shape[2]
    lb = cache_b_k.shape[2]
    assert t_s == 1 and t_p % B_MAX_WINDOW == 0 and la == A_WINDOW and lb == B_MAX_WINDOW
    rows_p = n_p * t_p
    rows_all = _round_up(rows_p + n_s, 256)
    xp = x_prompt.reshape(rows_p, d)
    xs = x_sample.reshape(n_s, d)
    mem_bf = mem_prompt.reshape(n_p * MEM_TOKENS, d).astype(BF16)
    window_rows = lambda cache: jnp.concatenate([cache[:, :, lb - w::dil] for w, dil in B_BRANCHES], axis=2)
    win_b_k, win_b_v = window_rows(cache_b_k), window_rows(cache_b_v)
    branches_b = tuple((i * BAND, 1) for i in range(len(B_BRANCHES)))
    no_sink = jnp.full((1, B_HEADS, HEAD_DIM), -jnp.inf, F32)
    zero_state = jnp.zeros((n_p, C_HEADS, HEAD_DIM, HEAD_DIM), F32)
    states = []
    for l in range(DEPTH):
        cpar = (c_mu[l], c_w0[l], c_w_up[l], c_a0[l], c_a_up[l], c_g_up[l], c_k_k[l], c_k_a[l])
        w_in_bf = w_in[l].astype(BF16)
        w_ab, w_fc = w_in_bf[:, :AB_COLS], w_in_bf[:, AB_COLS:]
        w_out_bf = w_out[l].astype(BF16)
        w_q_bf = w_mem_q[l].astype(BF16)
        w_o_bf = w_mem_o[l].astype(BF16)
        r_k = c_r_k[l].reshape(-1)

        xp_bf = xp.astype(BF16)
        qkv = _matmul(xp_bf, w_ab, 512, AB_COLS // 3).reshape(n_p, t_p, AB_COLS)
        fc = _matmul(xp_bf, w_fc, 512, C_SHIFT_W // 2).reshape(n_p, t_p, C_SHIFT_W)
        oa = _swa_attention(qkv, a_sink[l])
        ob = _dilated_attention(qkv)
        r, lw, k, v, kk, kka, g = _rwkv_pre(fc, None, *cpar, 128)
        y, p_wkv = _rwkv_chunked(r, lw, k, v, kk, kka, zero_state)
        oc = _rwkv_post(y, r, k, v, g, r_k, c_gn_g[l], c_gn_b[l], 128)
        xp = _out_proj_ln(oa.reshape(rows_p, -1), ob.reshape(rows_p, -1), oc.reshape(rows_p, -1), w_out_bf, xp,
                          ln_g[l, 0], ln_b[l, 0], 256)
        mk = _matmul(mem_bf, w_mem_k[l].astype(BF16), 512, 512).reshape(n_p, MEM_TOKENS, MEM_WIDTH)
        mv = _matmul(mem_bf, w_mem_v[l].astype(BF16), 512, 512).reshape(n_p, MEM_TOKENS, MEM_WIDTH)
        xp = _memory_attention(xp.reshape(n_p, t_p, d), mk, mv, w_q_bf, w_o_bf,
                               ln_g[l, 1], ln_b[l, 1], 256).reshape(rows_p, d)
        ka = qkv[:, t_p - A_WINDOW:, A_HEADS * HEAD_DIM:A_HEADS * HEAD_DIM + A_KV_HEADS * HEAD_DIM]
        va = qkv[:, t_p - A_WINDOW:, A_HEADS * HEAD_DIM + A_KV_HEADS * HEAD_DIM:A_COLS]
        kb = qkv[:, t_p - B_MAX_WINDOW:, A_COLS + B_HEADS * HEAD_DIM:A_COLS + 2 * B_HEADS * HEAD_DIM]
        vb = qkv[:, t_p - B_MAX_WINDOW:, A_COLS + 2 * B_HEADS * HEAD_DIM:]
        p_items = (ka.reshape(n_p, A_WINDOW, A_KV_HEADS, HEAD_DIM), va.reshape(n_p, A_WINDOW, A_KV_HEADS, HEAD_DIM),
                   kb.reshape(n_p, B_MAX_WINDOW, B_HEADS, HEAD_DIM), vb.reshape(n_p, B_MAX_WINDOW, B_HEADS, HEAD_DIM),
                   p_wkv, fc[:, -1],
                   mk.reshape(n_p, MEM_TOKENS, MEM_HEADS, MEM_HEAD_DIM),
                   mv.reshape(n_p, MEM_TOKENS, MEM_HEADS, MEM_HEAD_DIM))

        proj = _matmul(xs.astype(BF16), w_in_bf, 32, 512)
        qkv_s, fc_s = proj[:, :AB_COLS], proj[:, AB_COLS:]
        c0 = A_HEADS * HEAD_DIM
        c1 = c0 + A_KV_HEADS * HEAD_DIM
        ka_s = qkv_s[:, c0:c1].reshape(n_s, A_KV_HEADS, HEAD_DIM)
        va_s = qkv_s[:, c1:A_COLS].reshape(n_s, A_KV_HEADS, HEAD_DIM)
        b0 = A_COLS + B_HEADS * HEAD_DIM
        b1 = b0 + B_HEADS * HEAD_DIM
        kb_s = qkv_s[:, b0:b1].reshape(n_s, B_HEADS, HEAD_DIM)
        vb_s = qkv_s[:, b1:].reshape(n_s, B_HEADS, HEAD_DIM)
        q_g = qkv_s[:, :c0].reshape(n_s, A_KV_HEADS, A_GROUP, HEAD_DIM).transpose(0, 2, 1, 3)
        sink_g = jnp.broadcast_to(a_sink[l].reshape(A_KV_HEADS, A_GROUP).T[:, :, None],
                                  (A_GROUP, A_KV_HEADS, HEAD_DIM))
        oa_s = _cache_attention(q_g, cache_a_k, cache_a_v, l, ka_s, va_s, sink_g, ((0, 1),))
        oa_s = oa_s.transpose(0, 2, 1, 3).reshape(n_s, -1)
        ob_s = _cache_attention(qkv_s[:, A_COLS:b0].reshape(n_s, 1, B_HEADS, HEAD_DIM), win_b_k, win_b_v, l,
                                kb_s, vb_s, no_sink, branches_b).reshape(n_s, -1)
        r, lw, k, v, kk, kka, g = _rwkv_pre(fc_s[None], state_c_shift[l][None], *cpar, n_s)
        seq = lambda z: jnp.pad(z[0].transpose(1, 0, 2)[:, :, None], ((0, 0), (0, 0), (0, SCAN_BLOCK - 1), (0, 0)))
        y_s, s_wkv = _rwkv_scan(seq(r), seq(lw), seq(k), seq(kk), seq(kka), seq(v), state_c_wkv[l], 1)
        oc_s = _rwkv_post(y_s[:, :, 0].transpose(1, 0, 2)[None], r, k, v, g, r_k, c_gn_g[l], c_gn_b[l], n_s)
        xs = _out_proj_ln(oa_s.astype(BF16), ob_s.astype(BF16), oc_s[0], w_out_bf, xs, ln_g[l, 0], ln_b[l, 0], n_s)
        xs_pad = jnp.pad(xs[:, None], ((0, 0), (0, SUBLANES - 1), (0, 0)))
        xs = _memory_attention(xs_pad, cache_mem_k[l].reshape(n_s, MEM_TOKENS, MEM_WIDTH),
                               cache_mem_v[l].reshape(n_s, MEM_TOKENS, MEM_WIDTH), w_q_bf, w_o_bf,
                               ln_g[l, 1], ln_b[l, 1], SUBLANES)[:, 0]
        s_items = (ka_s[:, None], va_s[:, None], kb_s[:, None], vb_s[:, None], s_wkv, fc_s)

        x_all = jnp.concatenate([xp, xs, jnp.zeros((rows_all - rows_p - n_s, d), F32)], axis=0)
        x_all = _moe_layer(x_all, l, router_w, router_b, w_gate_up, b_gate_up, w_down, b_down,
                           ln_g[l, 2], ln_b[l, 2])
        xp, xs = x_all[:rows_p], x_all[rows_p:rows_p + n_s]
        states.append(p_items + s_items)

    stacked = [jnp.stack(z, axis=0) for z in zip(*states)]
    return (xp.reshape(n_p, t_p, d), xs.reshape(n_s, t_s, d), *stacked)
```

```python
import functools
import math

import jax
import jax.numpy as jnp
from jax import lax
from jax.experimental import pallas as pl
from jax.experimental.pallas import tpu as pltpu

F32 = jnp.float32
BF16 = jnp.bfloat16

D_MODEL = 2048
DEPTH = 4
HEAD_DIM = 64
A_HEADS = 8
A_KV_HEADS = 2
A_GROUP = 4
A_WINDOW = 128
B_HEADS = 8
B_BRANCHES = ((128, 1), (512, 4), (2048, 16))
B_MAX_WINDOW = 2048
C_HEADS = 16
C_WIDTH = C_HEADS * HEAD_DIM
C_LORA_W = 64
C_LORA_A = 64
C_LORA_G = 128
C_SHIFT_W = 3 * C_WIDTH + C_LORA_W + C_LORA_A + C_LORA_G
A_COLS = (A_HEADS + 2 * A_KV_HEADS) * HEAD_DIM
B_COLS = 3 * B_HEADS * HEAD_DIM
AB_COLS = A_COLS + B_COLS
MEM_TOKENS = 256
MEM_HEADS = 4
MEM_HEAD_DIM = 128
MEM_WIDTH = MEM_HEADS * MEM_HEAD_DIM
N_EXPERTS = 32
TOP_K = 4
D_FF = D_MODEL
SWIGLU_ALPHA = 1.702
SWIGLU_LIMIT = 7.0
BAND = 128
LN_EPS = 1e-5
GN_EPS = 64e-5
DECAY_SCALE = math.exp(-0.5)
DEEPNORM_ALPHA = (2 * DEPTH) ** 0.25

LANES = 128
SUBLANES = 8
VMEM_LIMIT = 48 * 1024 * 1024
SCAN_BLOCK = 128
RWKV_CHUNK = 64
MOE_BLOCK = 512
PAIR = LANES // HEAD_DIM


def _params(*sem):
    return pltpu.CompilerParams(dimension_semantics=sem, vmem_limit_bytes=VMEM_LIMIT)


def _round_up(x, m):
    return (x + m - 1) // m * m


def _sigmoid(x):
    return 1.0 / (1.0 + jnp.exp(-x))


def _layer_norm(z, g, b):
    mu = jnp.mean(z, axis=-1, keepdims=True)
    d = z - mu
    var = jnp.mean(d * d, axis=-1, keepdims=True)
    return d * lax.rsqrt(var + LN_EPS) * g + b


def _split3(x):
    h = x.astype(BF16)
    r = x - h.astype(F32)
    m = r.astype(BF16)
    l = (r - m.astype(F32)).astype(BF16)
    return h, m, l


def _bdot(a, b):
    return jnp.dot(a.astype(BF16), b.astype(BF16), preferred_element_type=F32)


def _bdot_nt(a, b):
    return lax.dot_general(a.astype(BF16), b.astype(BF16), (((1,), (1,)), ((), ())), preferred_element_type=F32)


def _bmm(a, b):
    return lax.dot_general(a.astype(BF16), b.astype(BF16), (((2,), (1,)), ((0,), (0,))), preferred_element_type=F32)


def _bmm_nt(a, b):
    return lax.dot_general(a.astype(BF16), b.astype(BF16), (((2,), (2,)), ((0,), (0,))), preferred_element_type=F32)


def _bmm_tn(a, b):
    return lax.dot_general(a.astype(BF16), b.astype(BF16), (((1,), (1,)), ((0,), (0,))), preferred_element_type=F32)


def _mm_body(a_ref, b_ref, o_ref):
    o_ref[...] = jnp.dot(a_ref[...], b_ref[...], preferred_element_type=F32)


def _matmul(a, b, tm, tn):
    m, k = a.shape
    n = b.shape[1]
    mp = _round_up(m, tm)
    if mp != m:
        a = jnp.pad(a, ((0, mp - m), (0, 0)))
    out = pl.pallas_call(
        _mm_body,
        grid=(n // tn, mp // tm),
        in_specs=[pl.BlockSpec((tm, k), lambda j, i: (i, 0)),
                  pl.BlockSpec((k, tn), lambda j, i: (0, j))],
        out_specs=pl.BlockSpec((tm, tn), lambda j, i: (i, j)),
        out_shape=jax.ShapeDtypeStruct((mp, n), F32),
        compiler_params=_params("parallel", "parallel"),
        name="mm",
    )(a, b)
    return out[:m]


def _out_proj_body(oa_ref, ob_ref, oc_ref, wa_ref, wb_ref, wc_ref, res_ref, g_ref, beta_ref, o_ref):
    dot = lambda a, w: jnp.dot(a[...], w[...], preferred_element_type=F32)
    h = dot(oa_ref, wa_ref) + dot(ob_ref, wb_ref) + dot(oc_ref, wc_ref)
    o_ref[...] = _layer_norm(DEEPNORM_ALPHA * res_ref[...] + h, g_ref[...], beta_ref[...])


def _out_proj_ln(oa, ob, oc, w, res, g, beta, tm):
    m = oa.shape[0]
    d = w.shape[1]
    ka, kc = oa.shape[1], oc.shape[1]
    row = lambda width: pl.BlockSpec((tm, width), lambda i: (i, 0))
    par = pl.BlockSpec((1, d), lambda i: (0, 0))
    return pl.pallas_call(
        _out_proj_body,
        grid=(m // tm,),
        in_specs=[row(ka), row(ka), row(kc),
                  pl.BlockSpec((ka, d), lambda i: (0, 0)), pl.BlockSpec((ka, d), lambda i: (1, 0)),
                  pl.BlockSpec((kc, d), lambda i: (1, 0)), row(d), par, par],
        out_specs=row(d),
        out_shape=jax.ShapeDtypeStruct((m, d), F32),
        compiler_params=_params("parallel"),
        name="out_proj_ln",
    )(oa, ob, oc, w, w, w, res, g.reshape(1, d), beta.reshape(1, d))


def _band_bias():
    row = lax.broadcasted_iota(jnp.int32, (BAND, 2 * BAND), 0)
    col = lax.broadcasted_iota(jnp.int32, (BAND, 2 * BAND), 1)
    dist = jnp.where(col < BAND, BAND + row - col, row - (col - BAND))
    return jnp.where((dist >= 0) & (dist <= BAND), 0.0, -jnp.inf), col < BAND


def _softmax_pv(s, v_bf, sink=None):
    m = jnp.max(s, axis=-1, keepdims=True)
    if sink is not None:
        m = jnp.maximum(m, sink)
    p = jnp.exp(s - m)
    denom = jnp.sum(p, axis=-1, keepdims=True)
    if sink is not None:
        denom = denom + jnp.exp(sink - m)
    num = jnp.dot(p.astype(BF16), v_bf, preferred_element_type=F32)
    return num / denom, m + jnp.log(denom)


def _swa_body(sink_ref, q_ref, kp_ref, kc_ref, vp_ref, vc_ref, o_ref):
    b = pl.program_id(1)
    scale = HEAD_DIM ** -0.5
    bias, is_prev = _band_bias()
    bias = bias + jnp.where(is_prev, jnp.where(b > 0, 0.0, -jnp.inf), 0.0)
    q = q_ref[0]
    outs = []
    for kv in range(A_KV_HEADS):
        sl = slice(kv * HEAD_DIM, (kv + 1) * HEAD_DIM)
        k_cat = jnp.concatenate([kp_ref[0][:, sl], kc_ref[0][:, sl]], axis=0).astype(BF16)
        v_cat = jnp.concatenate([vp_ref[0][:, sl], vc_ref[0][:, sl]], axis=0).astype(BF16)
        for g in range(A_GROUP):
            h = kv * A_GROUP + g
            s = _bdot_nt(q[:, h * HEAD_DIM:(h + 1) * HEAD_DIM], k_cat) * scale + bias
            o, _ = _softmax_pv(s, v_cat, sink_ref[h])
            outs.append(o)
    o_ref[0] = jnp.concatenate(outs, axis=-1).astype(o_ref.dtype)


def _swa_attention(qkv, sink):
    n, t, _ = qkv.shape
    qw = A_HEADS * HEAD_DIM
    kblk = qw // LANES
    prev = lambda c: pl.BlockSpec((1, BAND, LANES), lambda i, b: (i, jnp.maximum(b - 1, 0), c))
    cur = lambda c: pl.BlockSpec((1, BAND, LANES), lambda i, b: (i, b, c))
    qspec = pl.BlockSpec((1, BAND, qw), lambda i, b: (i, b, 0))
    return pl.pallas_call(
        _swa_body,
        grid=(n, t // BAND),
        in_specs=[pl.BlockSpec(memory_space=pltpu.SMEM), qspec, prev(kblk), cur(kblk), prev(kblk + 1), cur(kblk + 1)],
        out_specs=qspec,
        out_shape=jax.ShapeDtypeStruct((n, t, qw), BF16),
        compiler_params=_params("parallel", "parallel"),
        name="swa_attn",
    )(sink.astype(F32), qkv, qkv, qkv, qkv, qkv)


def _dilated_body(q_ref, kp_ref, kc_ref, vp_ref, vc_ref, o_ref, kcat, vcat, ob_scr, lse_scr):
    span = B_MAX_WINDOW
    sb = pl.program_id(2)
    scale = HEAD_DIM ** -0.5
    kcat[0:span] = kp_ref[0]
    kcat[span:2 * span] = kc_ref[0]
    vcat[0:span] = vp_ref[0]
    vcat[span:2 * span] = vc_ref[0]
    bias0, is_prev = _band_bias()
    no_prev_span = jnp.where(sb > 0, 0.0, -jnp.inf)
    first_head = lax.broadcasted_iota(jnp.int32, (BAND, LANES), 1) < HEAD_DIM
    for bi, (window, dil) in enumerate(B_BRANCHES):
        nsub = span // dil // BAND

        def sub(i, carry, bi=bi, dil=dil, nsub=nsub):
            r = i // nsub
            j = i % nsub
            q_start = r + dil * BAND * j
            k_start = span + q_start - dil * BAND
            if dil == 1:
                q_rows, k_rows = pl.ds(q_start, BAND), pl.ds(k_start, 2 * BAND)
            else:
                q_rows = pl.ds(q_start, BAND, stride=dil)
                k_rows = pl.ds(k_start, 2 * BAND, stride=dil)
            q2 = q_ref.at[0][q_rows, :]
            k2 = kcat[k_rows, :]
            v2 = vcat[k_rows, :]
            bias = bias0 + jnp.where(is_prev, jnp.where(j == 0, no_prev_span, 0.0), 0.0)
            k_bf = k2.astype(BF16)
            v_bf = v2.astype(BF16)
            o_a, lse_a = _softmax_pv(_bdot_nt(jnp.where(first_head, q2, 0.0), k_bf) * scale + bias, v_bf)
            o_b, lse_b = _softmax_pv(_bdot_nt(jnp.where(first_head, 0.0, q2), k_bf) * scale + bias, v_bf)
            ob_scr.at[bi][q_rows, :] = jnp.where(first_head, o_a, o_b)
            lse_scr.at[bi][q_rows, :] = jnp.where(first_head, lse_a, lse_b)
            return carry

        lax.fori_loop(0, span // BAND, sub, 0, unroll=4)
    l0, l1, l2 = lse_scr[0], lse_scr[1], lse_scr[2]
    m = jnp.maximum(jnp.maximum(l0, l1), l2)
    e0, e1, e2 = jnp.exp(l0 - m), jnp.exp(l1 - m), jnp.exp(l2 - m)
    tot = e0 + e1 + e2
    o_ref[0] = ((e0 / tot) * ob_scr[0] + (e1 / tot) * ob_scr[1] + (e2 / tot) * ob_scr[2]).astype(o_ref.dtype)


def _dilated_attention(qkv):
    n, t, _ = qkv.shape
    span = B_MAX_WINDOW
    width = B_HEADS * HEAD_DIM
    q0 = A_COLS // LANES
    nhp = width // LANES
    blk = lambda c0, prev: pl.BlockSpec(
        (1, span, LANES),
        (lambda i, p, s: (i, jnp.maximum(s - 1, 0), c0 + p)) if prev else (lambda i, p, s: (i, s, c0 + p)))
    return pl.pallas_call(
        _dilated_body,
        grid=(n, nhp, t // span),
        in_specs=[blk(q0, False), blk(q0 + nhp, True), blk(q0 + nhp, False),
                  blk(q0 + 2 * nhp, True), blk(q0 + 2 * nhp, False)],
        out_specs=pl.BlockSpec((1, span, LANES), lambda i, p, s: (i, s, p)),
        out_shape=jax.ShapeDtypeStruct((n, t, width), BF16),
        scratch_shapes=[pltpu.VMEM((2 * span, LANES), F32), pltpu.VMEM((2 * span, LANES), F32),
                        pltpu.VMEM((len(B_BRANCHES), span, LANES), F32),
                        pltpu.VMEM((len(B_BRANCHES), span, LANES), F32)],
        compiler_params=_params("parallel", "parallel", "parallel"),
        name="dilated_attn",
    )(qkv, qkv, qkv, qkv, qkv)


def _cache_attn_body(q_ref, kc_ref, vc_ref, kn_ref, vn_ref, sink_ref, o_ref, *, branches, group):
    scale = HEAD_DIM ** -0.5
    k_new = kn_ref[0]
    v_new = vn_ref[0]
    for g in range(group):
        q = q_ref[0, g]
        sink = sink_ref[g]
        s_n = jnp.sum(k_new * q, axis=-1, keepdims=True) * scale
        outs, lses = [], []
        for start, stride in branches:
            rows = pl.ds(start, BAND) if stride == 1 else pl.ds(start, BAND, stride=stride)
            k_c = kc_ref.at[0, 0][rows]
            v_c = vc_ref.at[0, 0][rows]
            s_c = jnp.sum(k_c * q[None], axis=-1, keepdims=True) * scale
            m = jnp.maximum(jnp.maximum(jnp.max(s_c, axis=0), s_n), sink)
            p_c = jnp.exp(s_c - m[None])
            p_n = jnp.exp(s_n - m)
            denom = jnp.sum(p_c, axis=0) + p_n + jnp.exp(sink - m)
            num = jnp.sum(p_c * v_c, axis=0) + p_n * v_new
            outs.append(num / denom)
            lses.append(m + jnp.log(denom))
        if len(branches) == 1:
            o = outs[0]
        else:
            m = functools.reduce(jnp.maximum, lses)
            es = [jnp.exp(l - m) for l in lses]
            tot = functools.reduce(lambda a, b: a + b, es)
            o = functools.reduce(lambda a, b: a + b, [(e / tot) * o_b for e, o_b in zip(es, outs)])
        o_ref[0, g] = o


def _cache_attention(q, k_cache, v_cache, layer, k_new, v_new, sink, branches):
    n, group, hk, hd = q.shape
    lc = k_cache.shape[2]
    cache = pl.BlockSpec((1, 1, lc, hk, hd), lambda i: (layer, i, 0, 0, 0), pipeline_mode=pl.Buffered(1))
    new = pl.BlockSpec((1, hk, hd), lambda i: (i, 0, 0))
    qspec = pl.BlockSpec((1, group, hk, hd), lambda i: (i, 0, 0, 0))
    return pl.pallas_call(
        functools.partial(_cache_attn_body, branches=branches, group=group),
        grid=(n,),
        in_specs=[qspec, cache, cache, new, new, pl.BlockSpec((group, hk, hd), lambda i: (0, 0, 0))],
        out_specs=qspec,
        out_shape=jax.ShapeDtypeStruct((n, group, hk, hd), F32),
        compiler_params=_params("parallel"),
        name="cache_attn",
    )(q, k_cache, v_cache, k_new, v_new, sink)


def _rwkv_pre_body(fc_ref, prev_ref, mu_ref, w0_ref, wup_ref, a0_ref, aup_ref, gup_ref, kk_w_ref, ka_w_ref,
                   r_ref, lw_ref, k_ref, v_ref, kk_ref, kka_ref, g_ref, *, shift_in_kernel):
    fc = fc_ref[0]
    if shift_in_kernel:
        last = jnp.where(pl.program_id(1) > 0, prev_ref[0][SUBLANES - 1:SUBLANES, :], 0.0)
        rowid = lax.broadcasted_iota(jnp.int32, fc.shape, 0)
        shifted = jnp.where(rowid == 0, last, pltpu.roll(fc, 1, 0))
    else:
        shifted = prev_ref[0]
    f = fc + (shifted - fc) * mu_ref[...]
    c = C_WIDTH
    r = f[:, 0:c]
    k = f[:, c:2 * c]
    v = f[:, 2 * c:3 * c]
    wd = f[:, 3 * c:3 * c + C_LORA_W]
    ad = f[:, 3 * c + C_LORA_W:3 * c + C_LORA_W + C_LORA_A]
    gd = f[:, 3 * c + C_LORA_W + C_LORA_A:]
    dot = lambda x, w_ref_: jnp.dot(x.astype(BF16), w_ref_[...], preferred_element_type=F32)
    log_decay = -DECAY_SCALE * _sigmoid(w0_ref[...] + dot(jnp.tanh(wd), wup_ref))
    a = _sigmoid(a0_ref[...] + dot(ad, aup_ref))
    g_ref[0] = dot(_sigmoid(gd), gup_ref)
    kk = k * kk_w_ref[...]
    k2 = k * (1.0 + (a - 1.0) * ka_w_ref[...])
    for h in range(C_HEADS):
        sl = slice(h * HEAD_DIM, (h + 1) * HEAD_DIM)
        kk_h = kk[:, sl]
        kk_h = kk_h / jnp.maximum(jnp.sqrt(jnp.sum(kk_h * kk_h, axis=-1, keepdims=True)), 1e-12)
        r_ref[0, h] = r[:, sl]
        lw_ref[0, h] = log_decay[:, sl]
        k_ref[0, h] = k2[:, sl]
        v_ref[0, h] = v[:, sl]
        kk_ref[0, h] = kk_h
        kka_ref[0, h] = kk_h * a[:, sl]


def _rwkv_pre(fc, shifted, mu, w0, w_up, a0, a_up, g_up, k_k, k_a, tm):
    n, t, _ = fc.shape
    c = C_WIDTH
    in_kernel = shifted is None
    fspec = pl.BlockSpec((1, tm, C_SHIFT_W), lambda i, j: (i, j, 0))
    if in_kernel:
        per = tm // SUBLANES
        pspec = pl.BlockSpec((1, SUBLANES, C_SHIFT_W), lambda i, j: (i, jnp.maximum(j * per - 1, 0), 0))
        shifted = fc
    else:
        pspec = fspec
    full = lambda a, b: pl.BlockSpec((a, b), lambda i, j: (0, 0))
    head = pl.BlockSpec((1, C_HEADS, tm, HEAD_DIM), lambda i, j: (i, 0, j, 0))
    hshape = jax.ShapeDtypeStruct((n, C_HEADS, t, HEAD_DIM), F32)
    return pl.pallas_call(
        functools.partial(_rwkv_pre_body, shift_in_kernel=in_kernel),
        grid=(n, t // tm),
        in_specs=[fspec, pspec, full(1, C_SHIFT_W), full(1, c), full(C_LORA_W, c),
                  full(1, c), full(C_LORA_A, c), full(C_LORA_G, c), full(1, c), full(1, c)],
        out_specs=[head] * 6 + [pl.BlockSpec((1, tm, c), lambda i, j: (i, j, 0))],
        out_shape=[hshape] * 6 + [jax.ShapeDtypeStruct((n, t, c), F32)],
        compiler_params=_params("parallel", "parallel"),
        name="rwkv_pre",
    )(fc, shifted, mu.reshape(1, -1), w0.reshape(1, c), w_up.astype(BF16), a0.reshape(1, c),
      a_up.astype(BF16), g_up.astype(BF16), k_k.reshape(1, c), k_a.reshape(1, c))


def _rwkv_intra_body(r_ref, lw_ref, k_ref, v_ref, kk_ref, kka_ref, q_ref, y1_ref, m_ref, n_ref):
    c = RWKV_CHUNK
    _, hb, tb, hd = r_ref.shape
    nb = hb * (tb // c)
    load = lambda ref: ref[0].reshape(nb, c, hd)
    row = lax.broadcasted_iota(jnp.int32, (nb, c, c), 1)
    col = lax.broadcasted_iota(jnp.int32, (nb, c, c), 2)
    incl = col <= row
    strict = col < row
    eye = col == row
    incl_bf = incl.astype(BF16)
    same = lambda bits: (row >> bits) == (col >> bits)
    lw = load(lw_ref)
    cum = functools.reduce(lambda a, b: a + b, [_bmm(incl_bf, part) for part in _split3(lw)])
    e_neg = jnp.exp(-cum)
    g_last = jnp.exp(cum[:, c - 1:c, :])
    kc = load(kk_ref) * jnp.exp(cum - lw)
    rh = load(r_ref) * jnp.exp(cum)
    kh = load(k_ref) * e_neg
    bh = load(kka_ref) * e_neg
    v = load(v_ref)
    a_kb = jnp.where(strict, _bmm_nt(kc, bh), 0.0)
    a_kk = jnp.where(strict, _bmm_nt(kc, kh), 0.0)
    a_rb = jnp.where(incl, _bmm_nt(rh, bh), 0.0)
    a_rk = jnp.where(incl, _bmm_nt(rh, kh), 0.0)
    l1 = jnp.where(same(3), a_kb, 0.0)
    l2 = _bmm(l1, l1)
    l4 = _bmm(l2, l2)
    t = jnp.where(eye, 1.0, 0.0) - l1
    t = t + _bmm(t, l2)
    t = t + _bmm(t, l4)
    for bits in (4, 5, 6):
        off = jnp.where(same(bits) & jnp.logical_not(same(bits - 1)), a_kb, 0.0)
        t = t - _bmm(_bmm(t, off), t)
    p = _bmm(t, kc)
    w1 = _bmm(t, _bmm(a_kk, v))
    q_ref[0] = (rh - _bmm(a_rb, p)).reshape(hb, tb, hd)
    y1_ref[0] = (_bmm(a_rk, v) - _bmm(a_rb, w1)).reshape(hb, tb, hd)
    bt = bh * g_last
    m_ref[0] = (jnp.where(eye, g_last, 0.0) - _bmm_tn(bt, p)).reshape(hb, tb // c, hd, hd)
    n_ref[0] = (_bmm_tn(kh * g_last, v) - _bmm_tn(bt, w1)).reshape(hb, tb // c, hd, hd)


def _rwkv_inter_body(q_ref, y1_ref, m_ref, n_ref, h0_ref, y_ref, hout_ref, h_scr):
    c = RWKV_CHUNK
    tb = pl.program_id(1)

    @pl.when(tb == 0)
    def _():
        h_scr[...] = h0_ref[0]

    state = h_scr[...]
    for ci in range(q_ref.shape[2] // c):
        sl = pl.ds(ci * c, c)
        y_ref[0, :, sl, :] = _bmm(q_ref[0, :, sl, :], state) + y1_ref[0, :, sl, :]
        state = _bmm(m_ref[0, :, ci], state) + n_ref[0, :, ci]
    h_scr[...] = state

    @pl.when(tb == pl.num_programs(1) - 1)
    def _():
        hout_ref[0] = state


def _rwkv_chunked(r, lw, k, v, kk, kka, s0, heads_per_step=8, tokens_intra=256, tokens_inter=256):
    n, h, t, hd = r.shape
    c = RWKV_CHUNK
    nc = t // c
    rows = pl.BlockSpec((1, heads_per_step, tokens_intra, hd), lambda i, g, j: (i, g, j, 0))
    mats = pl.BlockSpec((1, heads_per_step, tokens_intra // c, hd, hd), lambda i, g, j: (i, g, j, 0, 0))
    rshape = jax.ShapeDtypeStruct((n, h, t, hd), F32)
    mshape = jax.ShapeDtypeStruct((n, h, nc, hd, hd), F32)
    q, y1, m, nn = pl.pallas_call(
        _rwkv_intra_body,
        grid=(n, h // heads_per_step, t // tokens_intra),
        in_specs=[rows] * 6,
        out_specs=[rows, rows, mats, mats],
        out_shape=[rshape, rshape, mshape, mshape],
        compiler_params=_params("parallel", "parallel", "parallel"),
        name="rwkv_intra",
    )(r, lw, k, v, kk, kka)
    rows = pl.BlockSpec((1, h, tokens_inter, hd), lambda i, j: (i, 0, j, 0))
    mats = pl.BlockSpec((1, h, tokens_inter // c, hd, hd), lambda i, j: (i, 0, j, 0, 0))
    state = pl.BlockSpec((1, h, hd, hd), lambda i, j: (i, 0, 0, 0))
    y, h_out = pl.pallas_call(
        _rwkv_inter_body,
        grid=(n, t // tokens_inter),
        in_specs=[rows, rows, mats, mats, state],
        out_specs=[rows, state],
        out_shape=[rshape, jax.ShapeDtypeStruct((n, h, hd, hd), F32)],
        scratch_shapes=[pltpu.VMEM((h, hd, hd), F32)],
        compiler_params=_params("parallel", "arbitrary"),
        name="rwkv_inter",
    )(q, y1, m, nn, jnp.swapaxes(s0, -1, -2))
    return y, jnp.swapaxes(h_out, -1, -2)


def _rwkv_scan_body(r_ref, lw_ref, k_ref, kk_ref, kka_ref, vt_ref, s0_ref, yt_ref, sout_ref, s_scr, y_scr,
                    *, n_tokens):
    tb = pl.program_id(1)

    @pl.when(tb == 0)
    def _():
        s_scr[...] = s0_ref[0]

    y_scr[...] = jnp.zeros_like(y_scr)
    lane = lax.broadcasted_iota(jnp.int32, (1, 1, SCAN_BLOCK), 2)

    def step(u, carry):
        onehot = lane == u
        row = lambda ref: ref[0, :, pl.ds(u, 1), :]
        s = s_scr[...]
        v_col = jnp.sum(jnp.where(onehot, vt_ref[0], 0.0), axis=-1, keepdims=True)
        sa = jnp.sum(s * row(kk_ref), axis=-1, keepdims=True)
        s = s * jnp.exp(row(lw_ref)) - sa * row(kka_ref) + v_col * row(k_ref)
        y = jnp.sum(s * row(r_ref), axis=-1, keepdims=True)
        s_scr[...] = s
        y_scr[...] = jnp.where(onehot, y, y_scr[...])
        return carry

    lax.fori_loop(0, jnp.minimum(SCAN_BLOCK, n_tokens - tb * SCAN_BLOCK), step, 0)
    yt_ref[0] = y_scr[...]

    @pl.when(tb == pl.num_programs(1) - 1)
    def _():
        sout_ref[0] = s_scr[...]


def _rwkv_scan(r, lw, k, kk, kka, v, s0, n_tokens):
    n, h, tp, hd = r.shape
    vt = jnp.swapaxes(v, -1, -2)
    rows = pl.BlockSpec((1, h, SCAN_BLOCK, hd), lambda i, t: (i, 0, t, 0))
    cols = pl.BlockSpec((1, h, hd, SCAN_BLOCK), lambda i, t: (i, 0, 0, t))
    state = pl.BlockSpec((1, h, hd, hd), lambda i, t: (i, 0, 0, 0))
    yt, s_out = pl.pallas_call(
        functools.partial(_rwkv_scan_body, n_tokens=n_tokens),
        grid=(n, tp // SCAN_BLOCK),
        in_specs=[rows] * 5 + [cols, state],
        out_specs=[cols, state],
        out_shape=[jax.ShapeDtypeStruct((n, h, hd, tp), F32), jax.ShapeDtypeStruct((n, h, hd, hd), F32)],
        scratch_shapes=[pltpu.VMEM((h, hd, hd), F32), pltpu.VMEM((h, hd, SCAN_BLOCK), F32)],
        compiler_params=_params("parallel", "arbitrary"),
        name="rwkv_scan",
    )(r, lw, k, kk, kka, vt, s0)
    return jnp.swapaxes(yt, -1, -2), s_out


def _rwkv_post_body(y_ref, r_ref, k_ref, v_ref, g_ref, rk_ref, gng_ref, gnb_ref, o_ref):
    outs = []
    for h in range(C_HEADS):
        sl = slice(h * HEAD_DIM, (h + 1) * HEAD_DIM)
        y = y_ref[0, h]
        d = y - jnp.mean(y, axis=-1, keepdims=True)
        var = jnp.mean(d * d, axis=-1, keepdims=True)
        yn = d * lax.rsqrt(var + GN_EPS) * gng_ref[:, sl] + gnb_ref[:, sl]
        bonus = jnp.sum(r_ref[0, h] * k_ref[0, h] * rk_ref[:, sl], axis=-1, keepdims=True) * v_ref[0, h]
        outs.append(yn + bonus)
    o_ref[0] = (jnp.concatenate(outs, axis=-1) * g_ref[0]).astype(o_ref.dtype)


def _rwkv_post(y, r, k, v, g, r_k, gn_g, gn_b, tm):
    n, h, t, hd = y.shape
    c = C_WIDTH
    head = pl.BlockSpec((1, h, tm, hd), lambda i, j: (i, 0, j, 0))
    nat = pl.BlockSpec((1, tm, c), lambda i, j: (i, j, 0))
    par = pl.BlockSpec((1, c), lambda i, j: (0, 0))
    return pl.pallas_call(
        _rwkv_post_body,
        grid=(n, t // tm),
        in_specs=[head] * 4 + [nat] + [par] * 3,
        out_specs=nat,
        out_shape=jax.ShapeDtypeStruct((n, t, c), BF16),
        compiler_params=_params("parallel", "parallel"),
        name="rwkv_post",
    )(y, r, k, v, g, r_k.reshape(1, c), gn_g.reshape(1, c), gn_b.reshape(1, c))


def _mem_attn_body(x_ref, wq_ref, mk_ref, mv_ref, wo_ref, g_ref, beta_ref, o_ref):
    x = x_ref[0]
    q = jnp.dot(x.astype(BF16), wq_ref[...], preferred_element_type=F32)
    scale = MEM_HEAD_DIM ** -0.5
    heads = []
    for h in range(MEM_HEADS):
        sl = slice(h * MEM_HEAD_DIM, (h + 1) * MEM_HEAD_DIM)
        s = _bdot_nt(q[:, sl], mk_ref[0][:, sl]) * scale
        m = jnp.max(s, axis=-1, keepdims=True)
        p = jnp.exp(s - m)
        p = p / jnp.sum(p, axis=-1, keepdims=True)
        heads.append(_bdot(p, mv_ref[0][:, sl]))
    o = jnp.concatenate(heads, axis=-1)
    h_out = jnp.dot(o.astype(BF16), wo_ref[...], preferred_element_type=F32)
    o_ref[0] = _layer_norm(DEEPNORM_ALPHA * x + h_out, g_ref[...], beta_ref[...])


def _memory_attention(x, mem_k, mem_v, w_q, w_o, g, beta, tm):
    n, t, d = x.shape
    mt = mem_k.shape[1]
    full = lambda a, b: pl.BlockSpec((a, b), lambda i, j: (0, 0))
    xspec = pl.BlockSpec((1, tm, d), lambda i, j: (i, j, 0))
    mem = pl.BlockSpec((1, mt, MEM_WIDTH), lambda i, j: (i, 0, 0))
    return pl.pallas_call(
        _mem_attn_body,
        grid=(n, t // tm),
        in_specs=[xspec, full(d, MEM_WIDTH), mem, mem, full(MEM_WIDTH, d), full(1, d), full(1, d)],
        out_specs=xspec,
        out_shape=jax.ShapeDtypeStruct((n, t, d), F32),
        compiler_params=_params("parallel", "parallel"),
        name="mem_attn",
    )(x, w_q, mem_k, mem_v, w_o, g.reshape(1, d), beta.reshape(1, d))


def _router_body(x_ref, w_ref, b_ref, idx_ref, gate_ref):
    x = x_ref[...]
    w = w_ref[...]
    xh = x.astype(BF16)
    xl = (x - xh.astype(F32)).astype(BF16)
    wh = w.astype(BF16)
    wl = (w - wh.astype(F32)).astype(BF16)
    d = lambda a, b: jnp.dot(a, b, preferred_element_type=F32)
    logits = d(xh, wh) + d(xh, wl) + d(xl, wh) + b_ref[...]
    lane = lax.broadcasted_iota(jnp.int32, logits.shape, 1).astype(F32)
    neg = -jnp.inf
    work = jnp.where(lane < N_EXPERTS, logits, neg)
    idx_out = jnp.zeros(logits.shape, F32)
    val_out = jnp.full(logits.shape, neg, F32)
    for k in range(TOP_K):
        m = jnp.max(work, axis=-1, keepdims=True)
        first = jnp.min(jnp.where(work == m, lane, float(LANES)), axis=-1, keepdims=True)
        idx_out = jnp.where(lane == k, first, idx_out)
        val_out = jnp.where(lane == k, m, val_out)
        work = jnp.where(lane == first, neg, work)
    e = jnp.exp(val_out - jnp.max(val_out, axis=-1, keepdims=True))
    idx_ref[...] = idx_out.astype(jnp.int32)
    gate_ref[...] = e / jnp.sum(e, axis=-1, keepdims=True)


def _router(x, w, b, tm):
    rows, d = x.shape
    wp = jnp.pad(w, ((0, 0), (0, LANES - N_EXPERTS)))
    bp = jnp.pad(b, (0, LANES - N_EXPERTS)).reshape(1, LANES)
    out = pl.BlockSpec((tm, LANES), lambda i: (i, 0))
    idx, gates = pl.pallas_call(
        _router_body,
        grid=(rows // tm,),
        in_specs=[pl.BlockSpec((tm, d), lambda i: (i, 0)),
                  pl.BlockSpec((d, LANES), lambda i: (0, 0)),
                  pl.BlockSpec((1, LANES), lambda i: (0, 0))],
        out_specs=[out, out],
        out_shape=[jax.ShapeDtypeStruct((rows, LANES), jnp.int32), jax.ShapeDtypeStruct((rows, LANES), F32)],
        compiler_params=_params("parallel"),
        name="router",
    )(x, wp, bp)
    return idx[:, :TOP_K], gates[:, :TOP_K]


def _expert_up_body(be_ref, nu_ref, x_ref, wg_ref, wu_ref, bg_ref, bu_ref, h_ref):
    @pl.when(pl.program_id(1) < nu_ref[0])
    def _():
        x = x_ref[...].astype(BF16)
        gate = jnp.dot(x, wg_ref[0, 0].astype(BF16), preferred_element_type=F32) + bg_ref[0, 0]
        up = jnp.dot(x, wu_ref[0, 0].astype(BF16), preferred_element_type=F32) + bu_ref[0, 0]
        gate = jnp.minimum(gate, SWIGLU_LIMIT)
        up = jnp.clip(up, -SWIGLU_LIMIT, SWIGLU_LIMIT)
        h_ref[...] = ((up + 1.0) * gate * _sigmoid(SWIGLU_ALPHA * gate)).astype(h_ref.dtype)


def _expert_down_body(be_ref, nu_ref, h_ref, w_ref, b_ref, y_ref):
    @pl.when(pl.program_id(1) < nu_ref[0])
    def _():
        y_ref[...] = jnp.dot(h_ref[...], w_ref[0, 0].astype(BF16), preferred_element_type=F32) + b_ref[0, 0]


def _expert_ffn(rows, block_e, n_used, layer, w_gate_up, b_gate_up, w_down, b_down, tn):
    nr, d = rows.shape
    nb = nr // MOE_BLOCK
    nj = D_FF // tn
    b_gu = b_gate_up.reshape(DEPTH, N_EXPERTS, 1, 2 * D_FF)
    b_dn = b_down.reshape(DEPTH, N_EXPERTS, 1, d)
    hid = pl.pallas_call(
        _expert_up_body,
        grid_spec=pltpu.PrefetchScalarGridSpec(
            num_scalar_prefetch=2,
            grid=(nj, nb),
            in_specs=[pl.BlockSpec((MOE_BLOCK, d), lambda j, i, be, nu:(i, 0)),
                      pl.BlockSpec((1, 1, d, tn), lambda j, i, be, nu:(layer, be[i], 0, j)),
                      pl.BlockSpec((1, 1, d, tn), lambda j, i, be, nu:(layer, be[i], 0, j + nj)),
                      pl.BlockSpec((1, 1, 1, tn), lambda j, i, be, nu:(layer, be[i], 0, j)),
                      pl.BlockSpec((1, 1, 1, tn), lambda j, i, be, nu:(layer, be[i], 0, j + nj))],
            out_specs=pl.BlockSpec((MOE_BLOCK, tn), lambda j, i, be, nu:(i, j))),
        out_shape=jax.ShapeDtypeStruct((nr, D_FF), BF16),
        compiler_params=_params("parallel", "arbitrary"),
        name="expert_up",
    )(block_e, n_used, rows, w_gate_up, w_gate_up, b_gu, b_gu)
    tn = 2 * tn
    nk = d // tn
    return pl.pallas_call(
        _expert_down_body,
        grid_spec=pltpu.PrefetchScalarGridSpec(
            num_scalar_prefetch=2,
            grid=(nk, nb),
            in_specs=[pl.BlockSpec((MOE_BLOCK, D_FF), lambda j, i, be, nu:(i, 0)),
                      pl.BlockSpec((1, 1, D_FF, tn), lambda j, i, be, nu:(layer, be[i], 0, j)),
                      pl.BlockSpec((1, 1, 1, tn), lambda j, i, be, nu:(layer, be[i], 0, j))],
            out_specs=pl.BlockSpec((MOE_BLOCK, tn), lambda j, i, be, nu:(i, j))),
        out_shape=jax.ShapeDtypeStruct((nr, d), F32),
        compiler_params=_params("parallel", "arbitrary"),
        name="expert_down",
    )(block_e, n_used, hid, w_down, b_dn)


def _moe_combine_body(x_ref, y_ref, gate_ref, g_ref, beta_ref, o_ref):
    gates = gate_ref[...]
    acc = gates[:, 0:1] * y_ref[0]
    for k in range(1, TOP_K):
        acc = acc + gates[:, k:k + 1] * y_ref[k]
    o_ref[...] = _layer_norm(DEEPNORM_ALPHA * x_ref[...] + acc, g_ref[...], beta_ref[...])


def _moe_combine(x, y_k, gates, g, beta, tm):
    rows, d = x.shape
    gp = jnp.pad(gates, ((0, 0), (0, LANES - TOP_K)))
    xs = pl.BlockSpec((tm, d), lambda i: (i, 0))
    par = pl.BlockSpec((1, d), lambda i: (0, 0))
    return pl.pallas_call(
        _moe_combine_body,
        grid=(rows // tm,),
        in_specs=[xs, pl.BlockSpec((TOP_K, tm, d), lambda i: (0, i, 0)),
                  pl.BlockSpec((tm, LANES), lambda i: (i, 0)), par, par],
        out_specs=xs,
        out_shape=jax.ShapeDtypeStruct((rows, d), F32),
        compiler_params=_params("parallel"),
        name="moe_combine",
    )(x, y_k, gp, g.reshape(1, d), beta.reshape(1, d))


def _moe_layer(x, layer, router_w, router_b, w_gate_up, b_gate_up, w_down, b_down, g, beta):
    rows, d = x.shape
    top_idx, gates = _router(x, router_w[layer], router_b[layer], 256)
    n_assign = rows * TOP_K
    flat_e = top_idx.reshape(-1)
    onehot = (flat_e[:, None] == jnp.arange(N_EXPERTS, dtype=jnp.int32)[None, :]).astype(jnp.int32)
    running = jnp.cumsum(onehot, axis=0)
    counts = running[-1]
    padded = (counts + MOE_BLOCK - 1) // MOE_BLOCK * MOE_BLOCK
    pad_end = jnp.cumsum(padded)
    pad_start = pad_end - padded
    dest = jnp.sum(onehot * (running - 1 + pad_start[None, :]), axis=1)
    n_blocks = -(-n_assign // MOE_BLOCK) + N_EXPERTS
    row_tok = jnp.zeros((n_blocks * MOE_BLOCK,), jnp.int32).at[dest].set(
        jnp.arange(n_assign, dtype=jnp.int32) // TOP_K)
    block_start = jnp.arange(n_blocks, dtype=jnp.int32) * MOE_BLOCK
    block_e = jnp.minimum(jnp.sum((pad_end[None, :] <= block_start[:, None]).astype(jnp.int32), axis=1),
                          N_EXPERTS - 1)
    n_used = (pad_end[-1:] // MOE_BLOCK).astype(jnp.int32)
    y_rows = _expert_ffn(x[row_tok], block_e, n_used, layer, w_gate_up, b_gate_up, w_down, b_down, 512)
    return _moe_combine(x, y_rows[dest.reshape(rows, TOP_K).T], gates, g, beta, 256)


def kernel(x_prompt, x_sample, cache_a_k, cache_a_v, cache_b_k, cache_b_v, state_c_wkv, state_c_shift,
           cache_mem_k, cache_mem_v, mem_prompt, w_in, a_sink, c_mu, c_w0, c_w_up, c_a0, c_a_up, c_g_up,
           c_k_k, c_k_a, c_r_k, c_gn_g, c_gn_b, w_out, ln_g, ln_b, w_mem_q, w_mem_k, w_mem_v, w_mem_o,
           router_w, router_b, w_gate_up, b_gate_up, w_down, b_down):
    n_p, t_p, d = x_prompt.shape
    n_s, t_s, _ = x_sample.shape
    la = cache_a_k.shape[2]
    lb = cache_b_k.shape[2]
    assert t_s == 1 and t_p % B_MAX_WINDOW == 0 and la == A_WINDOW and lb == B_MAX_WINDOW
    rows_p = n_p * t_p
    rows_all = _round_up(rows_p + n_s, 256)
    xp = x_prompt.reshape(rows_p, d)
    xs = x_sample.reshape(n_s, d)
    mem_bf = mem_prompt.reshape(n_p * MEM_TOKENS, d).astype(BF16)
    branches_b = tuple((lb - w, dil) for w, dil in B_BRANCHES)
    no_sink = jnp.full((1, B_HEADS, HEAD_DIM), -jnp.inf, F32)
    zero_state = jnp.zeros((n_p, C_HEADS, HEAD_DIM, HEAD_DIM), F32)
    states = []
    for l in range(DEPTH):
        cpar = (c_mu[l], c_w0[l], c_w_up[l], c_a0[l], c_a_up[l], c_g_up[l], c_k_k[l], c_k_a[l])
        w_in_bf = w_in[l].astype(BF16)
        w_ab, w_fc = w_in_bf[:, :AB_COLS], w_in_bf[:, AB_COLS:]
        w_out_bf = w_out[l].astype(BF16)
        w_q_bf = w_mem_q[l].astype(BF16)
        w_o_bf = w_mem_o[l].astype(BF16)
        r_k = c_r_k[l].reshape(-1)

        xp_bf = xp.astype(BF16)
        qkv = _matmul(xp_bf, w_ab, 512, AB_COLS // 3).reshape(n_p, t_p, AB_COLS)
        fc = _matmul(xp_bf, w_fc, 512, C_SHIFT_W // 2).reshape(n_p, t_p, C_SHIFT_W)
        oa = _swa_attention(qkv, a_sink[l])
        ob = _dilated_attention(qkv)
        r, lw, k, v, kk, kka, g = _rwkv_pre(fc, None, *cpar, 128)
        y, p_wkv = _rwkv_chunked(r, lw, k, v, kk, kka, zero_state)
        oc = _rwkv_post(y, r, k, v, g, r_k, c_gn_g[l], c_gn_b[l], 128)
        xp = _out_proj_ln(oa.reshape(rows_p, -1), ob.reshape(rows_p, -1), oc.reshape(rows_p, -1), w_out_bf, xp,
                          ln_g[l, 0], ln_b[l, 0], 256)
        mk = _matmul(mem_bf, w_mem_k[l].astype(BF16), 512, 512).reshape(n_p, MEM_TOKENS, MEM_WIDTH)
        mv = _matmul(mem_bf, w_mem_v[l].astype(BF16), 512, 512).reshape(n_p, MEM_TOKENS, MEM_WIDTH)
        xp = _memory_attention(xp.reshape(n_p, t_p, d), mk, mv, w_q_bf, w_o_bf,
                               ln_g[l, 1], ln_b[l, 1], 256).reshape(rows_p, d)
        ka = qkv[:, t_p - A_WINDOW:, A_HEADS * HEAD_DIM:A_HEADS * HEAD_DIM + A_KV_HEADS * HEAD_DIM]
        va = qkv[:, t_p - A_WINDOW:, A_HEADS * HEAD_DIM + A_KV_HEADS * HEAD_DIM:A_COLS]
        kb = qkv[:, t_p - B_MAX_WINDOW:, A_COLS + B_HEADS * HEAD_DIM:A_COLS + 2 * B_HEADS * HEAD_DIM]
        vb = qkv[:, t_p - B_MAX_WINDOW:, A_COLS + 2 * B_HEADS * HEAD_DIM:]
        p_items = (ka.reshape(n_p, A_WINDOW, A_KV_HEADS, HEAD_DIM), va.reshape(n_p, A_WINDOW, A_KV_HEADS, HEAD_DIM),
                   kb.reshape(n_p, B_MAX_WINDOW, B_HEADS, HEAD_DIM), vb.reshape(n_p, B_MAX_WINDOW, B_HEADS, HEAD_DIM),
                   p_wkv, fc[:, -1],
                   mk.reshape(n_p, MEM_TOKENS, MEM_HEADS, MEM_HEAD_DIM),
                   mv.reshape(n_p, MEM_TOKENS, MEM_HEADS, MEM_HEAD_DIM))

        proj = _matmul(xs.astype(BF16), w_in_bf, 32, 512)
        qkv_s, fc_s = proj[:, :AB_COLS], proj[:, AB_COLS:]
        c0 = A_HEADS * HEAD_DIM
        c1 = c0 + A_KV_HEADS * HEAD_DIM
        ka_s = qkv_s[:, c0:c1].reshape(n_s, A_KV_HEADS, HEAD_DIM)
        va_s = qkv_s[:, c1:A_COLS].reshape(n_s, A_KV_HEADS, HEAD_DIM)
        b0 = A_COLS + B_HEADS * HEAD_DIM
        b1 = b0 + B_HEADS * HEAD_DIM
        kb_s = qkv_s[:, b0:b1].reshape(n_s, B_HEADS, HEAD_DIM)
        vb_s = qkv_s[:, b1:].reshape(n_s, B_HEADS, HEAD_DIM)
        q_g = qkv_s[:, :c0].reshape(n_s, A_KV_HEADS, A_GROUP, HEAD_DIM).transpose(0, 2, 1, 3)
        sink_g = jnp.broadcast_to(a_sink[l].reshape(A_KV_HEADS, A_GROUP).T[:, :, None],
                                  (A_GROUP, A_KV_HEADS, HEAD_DIM))
        oa_s = _cache_attention(q_g, cache_a_k, cache_a_v, l, ka_s, va_s, sink_g, ((0, 1),))
        oa_s = oa_s.transpose(0, 2, 1, 3).reshape(n_s, -1)
        ob_s = _cache_attention(qkv_s[:, A_COLS:b0].reshape(n_s, 1, B_HEADS, HEAD_DIM), cache_b_k, cache_b_v, l,
                                kb_s, vb_s, no_sink, branches_b).reshape(n_s, -1)
        r, lw, k, v, kk, kka, g = _rwkv_pre(fc_s[None], state_c_shift[l][None], *cpar, n_s)
        seq = lambda z: jnp.pad(z[0].transpose(1, 0, 2)[:, :, None], ((0, 0), (0, 0), (0, SCAN_BLOCK - 1), (0, 0)))
        y_s, s_wkv = _rwkv_scan(seq(r), seq(lw), seq(k), seq(kk), seq(kka), seq(v), state_c_wkv[l], 1)
        oc_s = _rwkv_post(y_s[:, :, 0].transpose(1, 0, 2)[None], r, k, v, g, r_k, c_gn_g[l], c_gn_b[l], n_s)
        xs = _out_proj_ln(oa_s.astype(BF16), ob_s.astype(BF16), oc_s[0], w_out_bf, xs, ln_g[l, 0], ln_b[l, 0], n_s)
        xs_pad = jnp.pad(xs[:, None], ((0, 0), (0, SUBLANES - 1), (0, 0)))
        xs = _memory_attention(xs_pad, cache_mem_k[l].reshape(n_s, MEM_TOKENS, MEM_WIDTH),
                               cache_mem_v[l].reshape(n_s, MEM_TOKENS, MEM_WIDTH), w_q_bf, w_o_bf,
                               ln_g[l, 1], ln_b[l, 1], SUBLANES)[:, 0]
        s_items = (ka_s[:, None], va_s[:, None], kb_s[:, None], vb_s[:, None], s_wkv, fc_s)

        x_all = jnp.concatenate([xp, xs, jnp.zeros((rows_all - rows_p - n_s, d), F32)], axis=0)
        x_all = _moe_layer(x_all, l, router_w, router_b, w_gate_up, b_gate_up, w_down, b_down,
                           ln_g[l, 2], ln_b[l, 2])
        xp, xs = x_all[:rows_p], x_all[rows_p:rows_p + n_s]
        states.append(p_items + s_items)

    stacked = [jnp.stack(z, axis=0) for z in zip(*states)]
    return (xp.reshape(n_p, t_p, d), xs.reshape(n_s, t_s, d), *stacked)
```
